```python
import jax
import jax.numpy as jnp
from jax import lax
import numpy as np

D_MODEL = 1024
BATCH = 2
SEQ = 16384
DEPTH = 4

GRID_W = 64
CTX_LEN = 256
CHUNK = 128
WINDOW = 128
HEAD_DIM = 64
N_GROUPS = 4
GROUP_WIDTH = D_MODEL // N_GROUPS
MIX_WIDTH = N_GROUPS * GROUP_WIDTH
N_HEADS = GROUP_WIDTH // HEAD_DIM
N_KV_HEADS = N_HEADS // 2
KV_WIDTH = N_KV_HEADS * HEAD_DIM
IN_SPLITS = (2 * GROUP_WIDTH, GROUP_WIDTH,
             GROUP_WIDTH, GROUP_WIDTH, GROUP_WIDTH, GROUP_WIDTH,
             GROUP_WIDTH, KV_WIDTH, KV_WIDTH, GROUP_WIDTH,
             GROUP_WIDTH, KV_WIDTH, KV_WIDTH, GROUP_WIDTH)
IN_WIDTH = 11 * GROUP_WIDTH + 4 * KV_WIDTH
ROPE_BASE = 10000.0
RMS_EPS = 1e-6
ATTN_SCALE = HEAD_DIM ** -0.5
NEG_INF = -1e30

kernel_name = 'hybrid_parallel_group_flow_block'


def rms_norm(x, gain):
    xf = x.astype(jnp.float32)
    y = xf * lax.rsqrt(jnp.mean(xf * xf, axis=-1, keepdims=True) + RMS_EPS)
    return (y * gain.astype(jnp.float32)).astype(x.dtype)


def modulate(x, gain, mod):
    shift, scale, gate = jnp.split(mod, 3, axis=-1)
    return rms_norm(x, gain) * (1.0 + scale) + shift, gate


def split_columns(z):
    idx = np.cumsum(IN_SPLITS)[:-1].tolist()
    return jnp.split(z, idx, axis=-1)


def to_heads(t):
    b, l, w = t.shape
    return t.reshape(b, l, w // HEAD_DIM, HEAD_DIM).transpose(0, 2, 1, 3)


def from_heads(t):
    b, n, l, d = t.shape
    return t.transpose(0, 2, 1, 3).reshape(b, l, n * d)


def axial_rope_tables(rows, dtype):
    row = jnp.broadcast_to(jnp.arange(rows, dtype=jnp.float32)[:, None], (rows, GRID_W)).reshape(-1)
    col = jnp.broadcast_to(jnp.arange(GRID_W, dtype=jnp.float32)[None, :], (rows, GRID_W)).reshape(-1)
    half = HEAD_DIM // 2
    inv_freq = 1.0 / (ROPE_BASE ** (jnp.arange(0, half, 2, dtype=jnp.float32) / half))
    ang_r = row[:, None] * inv_freq[None, :]
    ang_c = col[:, None] * inv_freq[None, :]
    ang = jnp.concatenate([ang_r, ang_r, ang_c, ang_c], axis=-1)
    return jnp.cos(ang).astype(dtype), jnp.sin(ang).astype(dtype)


def apply_axial_rope(x, cos, sin):
    def rot_half(u):
        u1, u2 = jnp.split(u, 2, axis=-1)
        return jnp.concatenate([-u2, u1], axis=-1)
    xr, xc = jnp.split(x, 2, axis=-1)
    return x * cos + jnp.concatenate([rot_half(xr), rot_half(xc)], axis=-1) * sin


def chunk_mlp_branch(uv, gate, mix, bias):
    u, v = jnp.split(jax.nn.gelu(uv), 2, axis=-1)
    b, l, _ = v.shape
    n = l // CHUNK
    vb = v.reshape(b, n, CHUNK, N_HEADS, HEAD_DIM)
    sv = jnp.einsum('hij,bnjhd->bnihd', mix, vb) + bias.T[None, None, :, :, None]
    return u * sv.reshape(b, l, GROUP_WIDTH) * jax.nn.silu(gate)


def retention_scan(q, k, v, log_gamma, state0, include_diag):
    b, h, l, dh = q.shape
    n = l // CHUNK
    qc = q.reshape(b, h, n, CHUNK, dh)
    kc = k.reshape(b, h, n, CHUNK, dh)
    vc = v.reshape(b, h, n, CHUNK, dh)
    lg = log_gamma[:, None]
    pos = jnp.arange(CHUNK, dtype=jnp.float32)
    diff = pos[:, None] - pos[None, :]
    keep = diff >= 0 if include_diag else diff > 0
    decay = jnp.where(keep, jnp.exp(lg[:, :, None] * jnp.where(keep, diff, 0.0)), 0.0).astype(q.dtype)
    q_decay = jnp.exp(lg * (pos + 1.0)).astype(q.dtype)
    k_decay = jnp.exp(lg * (CHUNK - 1.0 - pos)).astype(q.dtype)
    chunk_decay = jnp.exp(lg[:, 0] * CHUNK).astype(q.dtype)[None, :, None, None]
    scores = jnp.einsum('bhnid,bhnjd->bhnij', qc, kc) * decay[None, :, None]
    intra = jnp.einsum('bhnij,bhnjd->bhnid', scores, vc)
    kv = jnp.einsum('bhnjd,bhnje->nbhde', kc * k_decay[None, :, None, :, None], vc)

    def step(state, kv_c):
        return chunk_decay * state + kv_c, state

    final, prev = lax.scan(step, state0, kv)
    inter = jnp.einsum('bhnid,nbhde->bhnie', qc, prev) * q_decay[None, :, None, :, None]
    return (intra + inter).reshape(b, h, l, dh), final


def retention_bidir(q, k, v, lg_f, lg_b, init_f, init_b):
    y_f, st_f = retention_scan(q, k, v, lg_f, init_f, True)
    flip = lambda a: jnp.flip(a, axis=2)
    y_b, st_b = retention_scan(flip(q), flip(k), flip(v), lg_b, init_b, False)
    return y_f + flip(y_b), st_f, st_b


def blocked_attention(q, k, v):
    b, h, l, dh = q.shape
    hkv = k.shape[1]
    n = l // CHUNK
    qb = jnp.moveaxis(q.reshape(b, hkv, h // hkv, n, CHUNK, dh), 3, 0)

    def attend(q_blk):
        s = jnp.einsum('bgrid,bgjd->bgrij', q_blk, k, preferred_element_type=jnp.float32) * ATTN_SCALE
        p = jax.nn.softmax(s, axis=-1).astype(v.dtype)
        return jnp.einsum('bgrij,bgjd->bgrid', p, v)

    o = lax.map(attend, qb)
    return jnp.moveaxis(o, 0, 3).reshape(b, h, l, dh)


def window_attention(q, k, v, k_ctx, v_ctx, sink):
    b, h, l, dh = q.shape
    hkv = k.shape[1]
    rep = h // hkv
    n = l // CHUNK
    pad = ((0, 0), (0, 0), (WINDOW, WINDOW), (0, 0))
    kb = jnp.pad(k, pad).reshape(b, hkv, n + 2, CHUNK, dh)
    vb = jnp.pad(v, pad).reshape(b, hkv, n + 2, CHUNK, dh)
    kwin = jnp.concatenate([kb[:, :, :-2], kb[:, :, 1:-1], kb[:, :, 2:]], axis=3)
    vwin = jnp.concatenate([vb[:, :, :-2], vb[:, :, 1:-1], vb[:, :, 2:]], axis=3)
    qb = q.reshape(b, hkv, rep, n, CHUNK, dh)
    s_loc = jnp.einsum('bgrnid,bgnjd->bgrnij', qb, kwin, preferred_element_type=jnp.float32) * ATTN_SCALE
    qpos = jnp.arange(n)[:, None] * CHUNK + jnp.arange(CHUNK)[None, :]
    kpos = (jnp.arange(n)[:, None] - 1) * CHUNK + jnp.arange(3 * CHUNK)[None, :]
    valid = ((jnp.abs(qpos[:, :, None] - kpos[:, None, :]) <= WINDOW)
             & (kpos >= 0)[:, None, :] & (kpos < l)[:, None, :])
    s_loc = jnp.where(valid, s_loc, NEG_INF)
    s_ctx = jnp.einsum('bgrnid,bgjd->bgrnij', qb, k_ctx, preferred_element_type=jnp.float32) * ATTN_SCALE
    s_sink = jnp.broadcast_to(sink.astype(jnp.float32).reshape(hkv, rep)[None, :, :, None, None, None],
                              s_loc.shape[:-1] + (1,))
    p = jax.nn.softmax(jnp.concatenate([s_loc, s_ctx, s_sink], axis=-1), axis=-1).astype(v.dtype)
    nw = 3 * CHUNK
    lc = k_ctx.shape[2]
    o = (jnp.einsum('bgrnij,bgnjd->bgrnid', p[..., :nw], vwin)
         + jnp.einsum('bgrnij,bgjd->bgrnid', p[..., nw:nw + lc], v_ctx))
    return o.reshape(b, h, l, dh)


def context_sink_attention(q, k, v, sink):
    b, h, lc, dh = q.shape
    hkv = k.shape[1]
    rep = h // hkv
    qg = q.reshape(b, hkv, rep, lc, dh)
    s = jnp.einsum('bgrid,bgjd->bgrij', qg, k, preferred_element_type=jnp.float32) * ATTN_SCALE
    s_sink = jnp.broadcast_to(sink.astype(jnp.float32).reshape(hkv, rep)[None, :, :, None, None], s.shape[:-1] + (1,))
    p = jax.nn.softmax(jnp.concatenate([s, s_sink], axis=-1), axis=-1)[..., :lc].astype(v.dtype)
    return jnp.einsum('bgrij,bgjd->bgrid', p, v).reshape(b, h, lc, dh)


def setup_inputs(seed: int = 0) -> dict:
    key = jax.random.key(seed)
    ks = jax.random.split(key, 20)
    f32 = jnp.float32
    nrm = lambda k, shape, s: jax.random.normal(k, shape, f32) * s
    base_rate = jnp.asarray(np.log(-np.log(1.0 - 2.0 ** (-5.0 - np.arange(N_HEADS)))), f32)
    return {
        'x': nrm(ks[0], (BATCH, SEQ, D_MODEL), 1.0),
        'c': nrm(ks[1], (BATCH, D_MODEL), 1.0),
        'ctx': nrm(ks[2], (BATCH, CTX_LEN, D_MODEL), 1.0),
        'c_ctx': nrm(ks[3], (D_MODEL,), 1.0),
        'norm_gain': 1.0 + nrm(ks[4], (DEPTH, D_MODEL), 0.02),
        'w_mod': nrm(ks[5], (DEPTH, D_MODEL, 3 * D_MODEL), 0.5 * D_MODEL ** -0.5),
        'b_mod': nrm(ks[6], (DEPTH, 3 * D_MODEL), 0.01),
        'w_in': nrm(ks[7], (DEPTH, D_MODEL, IN_WIDTH), D_MODEL ** -0.5),
        'w_out': nrm(ks[8], (DEPTH, MIX_WIDTH, D_MODEL), MIX_WIDTH ** -0.5),
        'mlp_mix': nrm(ks[9], (DEPTH, N_HEADS, CHUNK, CHUNK), CHUNK ** -0.5),
        'mlp_bias': 1.0 + nrm(ks[10], (DEPTH, N_HEADS, CHUNK), 0.1),
        'ret_decay_fwd': base_rate + nrm(ks[11], (DEPTH, N_HEADS), 0.05),
        'ret_decay_bwd': base_rate + nrm(ks[12], (DEPTH, N_HEADS), 0.05),
        'ret_norm': 1.0 + nrm(ks[13], (DEPTH, N_HEADS, HEAD_DIM), 0.02),
        'attn_q_norm': 1.0 + nrm(ks[14], (DEPTH, HEAD_DIM), 0.02),
        'attn_k_norm': 1.0 + nrm(ks[15], (DEPTH, HEAD_DIM), 0.02),
        'swa_q_norm': 1.0 + nrm(ks[16], (DEPTH, HEAD_DIM), 0.02),
        'swa_k_norm': 1.0 + nrm(ks[17], (DEPTH, HEAD_DIM), 0.02),
        'swa_sink': nrm(ks[18], (DEPTH, N_HEADS), 0.5),
    }


def reference(x, c, ctx, c_ctx, norm_gain, w_mod, b_mod, w_in, w_out, mlp_mix, mlp_bias,
              ret_decay_fwd, ret_decay_bwd, ret_norm, attn_q_norm, attn_k_norm,
              swa_q_norm, swa_k_norm, swa_sink):
    b, l, _ = x.shape
    rows = l // GRID_W
    cos, sin = axial_rope_tables(rows, x.dtype)
    cond_lat = jax.nn.silu(c)[:, None, :]
    cond_ctx = jax.nn.silu(c_ctx)[None, None, :]
    k_scale = HEAD_DIM ** -0.5
    xc = ctx
    for i in range(DEPTH):
        with_ctx = i < DEPTH - 1
        h, gate_lat = modulate(x, norm_gain[i], cond_lat @ w_mod[i] + b_mod[i])
        hc, gate_ctx = modulate(xc, norm_gain[i], cond_ctx @ w_mod[i] + b_mod[i])
        (a_uv, a_g, r_q, r_k, r_v, r_g, g_q, g_k, g_v, g_g,
         s_q, s_k, s_v, s_g) = split_columns(h @ w_in[i])
        (a_uv_c, a_g_c, r_q_c, r_k_c, r_v_c, r_g_c, g_q_c, g_k_c, g_v_c, g_g_c,
         s_q_c, s_k_c, s_v_c, s_g_c) = split_columns(hc @ w_in[i])

        mlp_l = chunk_mlp_branch(a_uv, a_g, mlp_mix[i], mlp_bias[i])

        lg_f = -jnp.exp(ret_decay_fwd[i].astype(jnp.float32))
        lg_b = -jnp.exp(ret_decay_bwd[i].astype(jnp.float32))
        zero = jnp.zeros((b, N_HEADS, HEAD_DIM, HEAD_DIM), x.dtype)
        ry_c, st_f, st_b = retention_bidir(to_heads(r_q_c), to_heads(r_k_c) * k_scale, to_heads(r_v_c),
                                           lg_f, lg_b, zero, zero)
        ry_l, _, _ = retention_bidir(to_heads(r_q), to_heads(r_k) * k_scale, to_heads(r_v),
                                     lg_f, lg_b, st_f, st_b)
        ret_gain = ret_norm[i][:, None, :]
        ret_l = from_heads(rms_norm(ry_l, ret_gain)) * jax.nn.silu(r_g)

        gk_c = rms_norm(to_heads(g_k_c), attn_k_norm[i])
        gv_c = to_heads(g_v_c)
        gq = apply_axial_rope(rms_norm(to_heads(g_q), attn_q_norm[i]), cos, sin)
        gk = apply_axial_rope(rms_norm(to_heads(g_k), attn_k_norm[i]), cos, sin)
        ga_l = from_heads(blocked_attention(gq, jnp.concatenate([gk_c, gk], axis=2),
                                            jnp.concatenate([gv_c, to_heads(g_v)], axis=2))) * jax.nn.silu(g_g)

        sk_c = rms_norm(to_heads(s_k_c), swa_k_norm[i])
        sv_c = to_heads(s_v_c)
        sq = apply_axial_rope(rms_norm(to_heads(s_q), swa_q_norm[i]), cos, sin)
        sk = apply_axial_rope(rms_norm(to_heads(s_k), swa_k_norm[i]), cos, sin)
        swa_l = from_heads(window_attention(sq, sk, to_heads(s_v), sk_c, sv_c, swa_sink[i])) * jax.nn.silu(s_g)

        if with_ctx:
            mlp_c = chunk_mlp_branch(a_uv_c, a_g_c, mlp_mix[i], mlp_bias[i])
            ret_c = from_heads(rms_norm(ry_c, ret_gain)) * jax.nn.silu(r_g_c)
            ga_c = from_heads(blocked_attention(rms_norm(to_heads(g_q_c), attn_q_norm[i]), gk_c, gv_c)) * jax.nn.silu(g_g_c)
            swa_c = from_heads(context_sink_attention(rms_norm(to_heads(s_q_c), swa_q_norm[i]), sk_c, sv_c,
                                                      swa_sink[i])) * jax.nn.silu(s_g_c)
            xc = xc + gate_ctx * (jnp.concatenate([mlp_c, ret_c, ga_c, swa_c], axis=-1) @ w_out[i])

        x = x + gate_lat * (jnp.concatenate([mlp_l, ret_l, ga_l, swa_l], axis=-1) @ w_out[i])
    return x
```

```python
import functools

import numpy as np
import jax
import jax.numpy as jnp
from jax import lax
from jax.experimental import pallas as pl
from jax.experimental.pallas import tpu as pltpu

F32 = jnp.float32
BF16 = jnp.bfloat16

GRID_W = 64
CHUNK = 128
WINDOW = 128
HEAD_DIM = 64
N_HEADS = 4
GROUP_WIDTH = N_HEADS * HEAD_DIM
N_KV = 2
KV_WIDTH = N_KV * HEAD_DIM
ROPE_BASE = 10000.0
RMS_EPS = 1e-6
ATTN_SCALE = HEAD_DIM ** -0.5
NEG_INF = -1e30

_GW = GROUP_WIDTH
C_AUV, C_AG = 0, 2 * _GW
C_RQ, C_RK, C_RV, C_RG = 3 * _GW, 4 * _GW, 5 * _GW, 6 * _GW
C_GQ = 7 * _GW
C_GK = C_GQ + _GW
C_GV = C_GK + KV_WIDTH
C_GG = C_GV + KV_WIDTH
C_SQ = C_GG + _GW
C_SK = C_SQ + _GW
C_SV = C_SK + KV_WIDTH
C_SG = C_SV + KV_WIDTH
IN_WIDTH = C_SG + _GW
QKV_WIDTH = 2 * _GW + 2 * KV_WIDTH

TOKEN_TILE = 256
ATTN_TQ = 256
ATTN_TK = 256
VMEM_LIMIT = 48 * 1024 * 1024


def _silu(x):
    return x / (1.0 + jnp.exp(-x))


def _head_mask(width, head):
    lane = lax.broadcasted_iota(jnp.int32, (1, width), 1)
    return (lane // HEAD_DIM) == head


def _group_mean_sq(y, bd):
    sq = y * y
    hi = sq.astype(BF16)
    lo = (sq - hi.astype(F32)).astype(BF16)
    return (jnp.dot(hi, bd, preferred_element_type=F32)
            + jnp.dot(lo, bd, preferred_element_type=F32))


def _mod_kernel(cond_ref, w_ref, b_ref, o_ref):
    a = _silu(cond_ref[...])
    o_ref[0] = jnp.dot(a, w_ref[0], preferred_element_type=F32) + b_ref[0]


def _modulation(cond, w_mod, b_mod):
    depth, d, d3 = w_mod.shape
    nblk = d3 // d
    return pl.pallas_call(
        _mod_kernel,
        out_shape=jax.ShapeDtypeStruct((depth, 8, d3), F32),
        grid=(depth, nblk),
        in_specs=[
            pl.BlockSpec((8, d), lambda i, j: (0, 0)),
            pl.BlockSpec((1, d, d), lambda i, j: (i, 0, j)),
            pl.BlockSpec((1, 1, d), lambda i, j: (i, 0, j)),
        ],
        out_specs=pl.BlockSpec((1, 8, d), lambda i, j: (i, 0, j)),
        compiler_params=pltpu.CompilerParams(
            dimension_semantics=("arbitrary", "arbitrary"), vmem_limit_bytes=VMEM_LIMIT),
        name="modulation",
    )(cond, w_mod, b_mod.reshape(depth, 1, d3))


def _inproj_kernel(x_ref, mod_ref, gain_ref, w_ref, mix_ref, mbias_ref, bd_ref,
                   cos_ref, sina_ref, sinb_ref, qkg_ref,
                   mlp_ref, r_ref, g_ref, s_ref, *, ctx_tiles, d_model):
    b = pl.program_id(0)
    t = pl.program_id(1)
    d = d_model
    row = jnp.where(t < ctx_tiles, 2, b)
    mod = mod_ref[pl.ds(row, 1), :]
    shift = mod[:, :d]
    scale = mod[:, d:2 * d]
    x = x_ref[0]
    ms = jnp.mean(x * x, axis=-1, keepdims=True)
    h = x * lax.rsqrt(ms + RMS_EPS) * gain_ref[...]
    h = (h * (1.0 + scale) + shift).astype(BF16)

    def proj(lo, width):
        return jnp.dot(h, w_ref[:, lo:lo + width], preferred_element_type=F32)

    bd = bd_ref[...]
    cos = cos_ref[...]
    sina = sina_ref[...]
    sinb = sinb_ref[...]

    def norm_rope(y, gain_row):
        yn = y * lax.rsqrt(_group_mean_sq(y, bd) + RMS_EPS) * gain_row
        return (yn * cos + pltpu.roll(yn, 128 - 16, 1) * sina + pltpu.roll(yn, 16, 1) * sinb)

    uv = jax.nn.gelu(proj(C_AUV, 2 * _GW), approximate=True)
    u = uv[:, :_GW]
    v = uv[:, _GW:].astype(BF16)
    ag = _silu(proj(C_AG, _GW))
    tm = x.shape[0]
    masks = [_head_mask(_GW, hh) for hh in range(N_HEADS)]
    for c in range(tm // CHUNK):
        rows = slice(c * CHUNK, (c + 1) * CHUNK)
        vc = v[rows]
        sv = mbias_ref[...]
        for hh in range(N_HEADS):
            mixed = jnp.dot(mix_ref[hh], vc, preferred_element_type=F32)
            sv = sv + jnp.where(masks[hh], mixed, 0.0)
        mlp_ref[0, rows, :] = (u[rows] * sv * ag[rows]).astype(mlp_ref.dtype)

    r_ref[0, :, 0:_GW] = proj(C_RQ, _GW)
    r_ref[0, :, _GW:2 * _GW] = proj(C_RK, _GW) * ATTN_SCALE
    r_ref[0, :, 2 * _GW:3 * _GW] = proj(C_RV, _GW)
    r_ref[0, :, 3 * _GW:4 * _GW] = _silu(proj(C_RG, _GW))

    for out_ref, cq, ck, cv, cg, gi in ((g_ref, C_GQ, C_GK, C_GV, C_GG, 0),
                                        (s_ref, C_SQ, C_SK, C_SV, C_SG, 2)):
        qg = qkg_ref[gi:gi + 1, :]
        kg = qkg_ref[gi + 1:gi + 2, :]
        q = proj(cq, _GW)
        out_ref[0, :, 0:128] = norm_rope(q[:, :128], qg).astype(out_ref.dtype)
        out_ref[0, :, 128:256] = norm_rope(q[:, 128:], qg).astype(out_ref.dtype)
        out_ref[0, :, 256:384] = norm_rope(proj(ck, KV_WIDTH), kg).astype(out_ref.dtype)
        out_ref[0, :, 384:512] = proj(cv, KV_WIDTH).astype(out_ref.dtype)
        out_ref[0, :, 512:768] = _silu(proj(cg, _GW)).astype(out_ref.dtype)


def _inproj(xs, mod, gain, w_in, mix, mbias, bd, cos, sina, sinb, qkg, *, ctx_len):
    bsz, t_len, d = xs.shape
    tm = TOKEN_TILE
    nt = t_len // tm
    const2 = lambda b, t: (0, 0)
    tok = lambda b, t: (b, t, 0)
    tab = lambda b, t: (t, 0)
    return pl.pallas_call(
        functools.partial(_inproj_kernel, ctx_tiles=ctx_len // tm, d_model=d),
        out_shape=(
            jax.ShapeDtypeStruct((bsz, t_len, _GW), BF16),
            jax.ShapeDtypeStruct((bsz, t_len, 4 * _GW), F32),
            jax.ShapeDtypeStruct((bsz, t_len, QKV_WIDTH), BF16),
            jax.ShapeDtypeStruct((bsz, t_len, QKV_WIDTH), BF16),
        ),
        grid=(bsz, nt),
        in_specs=[
            pl.BlockSpec((1, tm, d), tok),
            pl.BlockSpec((8, 3 * d), const2),
            pl.BlockSpec((1, d), const2),
            pl.BlockSpec((d, IN_WIDTH), const2),
            pl.BlockSpec((N_HEADS, CHUNK, CHUNK), lambda b, t: (0, 0, 0)),
            pl.BlockSpec((CHUNK, _GW), const2),
            pl.BlockSpec((128, 128), const2),
            pl.BlockSpec((tm, 128), tab),
            pl.BlockSpec((tm, 128), tab),
            pl.BlockSpec((tm, 128), tab),
            pl.BlockSpec((4, 128), const2),
        ],
        out_specs=(
            pl.BlockSpec((1, tm, _GW), tok),
            pl.BlockSpec((1, tm, 4 * _GW), tok),
            pl.BlockSpec((1, tm, QKV_WIDTH), tok),
            pl.BlockSpec((1, tm, QKV_WIDTH), tok),
        ),
        compiler_params=pltpu.CompilerParams(
            dimension_semantics=("arbitrary", "arbitrary"), vmem_limit_bytes=VMEM_LIMIT),
        name="inproj",
    )(xs, mod, gain, w_in, mix, mbias, bd, cos, sina, sinb, qkg)


def _ret_direction(q, k, v, dec_ref, qdec, kdec, cdec, state_ref, y_ref):
    qb = q.astype(BF16)
    kb = k.astype(BF16)
    vb = v.astype(BF16)
    state = state_ref[...]
    y = jnp.dot(qb, state.astype(BF16), preferred_element_type=F32) * qdec
    for hh in range(N_HEADS):
        mask = _head_mask(_GW, hh)
        qh = jnp.where(mask, q, 0.0).astype(BF16)
        sc = lax.dot_general(qh, kb, (((1,), (1,)), ((), ())), preferred_element_type=F32)
        sc = (sc * dec_ref[hh]).astype(BF16)
        y = y + jnp.where(mask, jnp.dot(sc, vb, preferred_element_type=F32), 0.0)
    y_ref[0] = y
    kd = (k * kdec).astype(BF16)
    kv = lax.dot_general(kd, vb, (((0,), (0,)), ((), ())), preferred_element_type=F32)
    row_head = lax.broadcasted_iota(jnp.int32, (_GW, _GW), 0) // HEAD_DIM
    col_head = lax.broadcasted_iota(jnp.int32, (_GW, _GW), 1) // HEAD_DIM
    state_ref[...] = state * cdec + jnp.where(row_head == col_head, kv, 0.0)


def _ret_kernel(qf_ref, kf_ref, vf_ref, qb_ref, kb_ref, vb_ref,
                decf_ref, decb_ref, vec_ref, yf_ref, yb_ref, sf_ref, sb_ref):
    @pl.when(pl.program_id(1) == 0)
    def _():
        sf_ref[...] = jnp.zeros_like(sf_ref)
        sb_ref[...] = jnp.zeros_like(sb_ref)

    _ret_direction(qf_ref[0], kf_ref[0], vf_ref[0], decf_ref,
                   vec_ref[0], vec_ref[1], vec_ref[2, 0:1, :], sf_ref, yf_ref)
    _ret_direction(qb_ref[0], kb_ref[0], vb_ref[0], decb_ref,
                   vec_ref[3], vec_ref[4], vec_ref[5, 0:1, :], sb_ref, yb_ref)


def _retention(r, decf, decb, vecs, *, ctx_len):
    bsz, t_len, _ = r.shape
    nc = t_len // CHUNK
    cc = ctx_len // CHUNK

    def bwd_chunk(t):
        return jnp.where(t < cc, cc - 1 - t, nc - 1 + cc - t)

    def fwd(col):
        return pl.BlockSpec((1, CHUNK, _GW), lambda b, t: (b, t, col))

    def bwd(col):
        return pl.BlockSpec((1, CHUNK, _GW), lambda b, t: (b, bwd_chunk(t), col))

    return pl.pallas_call(
        _ret_kernel,
        out_shape=(jax.ShapeDtypeStruct((bsz, t_len, _GW), F32),
                   jax.ShapeDtypeStruct((bsz, t_len, _GW), F32)),
        grid=(bsz, nc),
        in_specs=[fwd(0), fwd(1), fwd(2), bwd(0), bwd(1), bwd(2),
                  pl.BlockSpec((N_HEADS, CHUNK, CHUNK), lambda b, t: (0, 0, 0)),
                  pl.BlockSpec((N_HEADS, CHUNK, CHUNK), lambda b, t: (0, 0, 0)),
                  pl.BlockSpec((6, CHUNK, _GW), lambda b, t: (0, 0, 0))],
        out_specs=(pl.BlockSpec((1, CHUNK, _GW), lambda b, t: (b, t, 0)),
                   pl.BlockSpec((1, CHUNK, _GW), lambda b, t: (b, bwd_chunk(t), 0))),
        scratch_shapes=[pltpu.VMEM((_GW, _GW), F32), pltpu.VMEM((_GW, _GW), F32)],
        compiler_params=pltpu.CompilerParams(
            dimension_semantics=("arbitrary", "arbitrary"), vmem_limit_bytes=VMEM_LIMIT),
        name="retention",
    )(r, r, r, r, r, r, decf, decb, vecs)


def _retention_tables(lg_f, lg_b):
    pos = jnp.arange(CHUNK, dtype=F32)
    diff = pos[:, None] - pos[None, :]
    keep_f = diff >= 0
    dec_f = jnp.where(keep_f, jnp.exp(lg_f[:, None, None] * jnp.where(keep_f, diff, 0.0)), 0.0)
    keep_b = diff < 0
    dec_b = jnp.where(keep_b, jnp.exp(lg_b[:, None, None] * jnp.where(keep_b, -diff, 0.0)), 0.0)

    def lanes(tab):
        return jnp.repeat(tab.T, HEAD_DIM, axis=1)

    qdec_f = lanes(jnp.exp(lg_f[:, None] * (pos + 1.0)))
    kdec_f = lanes(jnp.exp(lg_f[:, None] * (CHUNK - 1.0 - pos)))
    qdec_b = lanes(jnp.exp(lg_b[:, None] * (CHUNK - pos)))
    kdec_b = lanes(jnp.exp(lg_b[:, None] * pos))
    cdec_f = jnp.broadcast_to(jnp.repeat(jnp.exp(lg_f * CHUNK), HEAD_DIM)[None, :], (CHUNK, _GW))
    cdec_b = jnp.broadcast_to(jnp.repeat(jnp.exp(lg_b * CHUNK), HEAD_DIM)[None, :], (CHUNK, _GW))
    vecs = jnp.stack([qdec_f, kdec_f, cdec_f, qdec_b, kdec_b, cdec_b]).astype(F32)
    return dec_f.astype(F32), dec_b.astype(F32), vecs


def _gattn_kernel(q_ref, kt_ref, va_ref, gate_ref, o_ref, m_ref, acc_ref, *,
                  tk, n_kt, ctx_qtiles, ctx_kt):
    qi = pl.program_id(2)
    tq = q_ref.shape[1]
    q = q_ref[0]
    q2 = jnp.concatenate([q[:, :HEAD_DIM], q[:, HEAD_DIM:]], axis=0)
    m_ref[...] = jnp.full(m_ref.shape, NEG_INF, F32)
    acc_ref[...] = jnp.zeros_like(acc_ref)
    trips = jnp.where(qi < ctx_qtiles, ctx_kt, n_kt)

    def body(kt, carry):
        start = pl.multiple_of(kt * tk, tk)
        k = kt_ref[0, 0, :, pl.ds(start, tk)]
        s = jnp.dot(q2, k, preferred_element_type=F32)
        m_prev = m_ref[...]
        m_new = jnp.maximum(m_prev, jnp.max(s, axis=-1, keepdims=True))
        alpha = jnp.exp(m_prev - m_new)
        p = jnp.exp(s - m_new).astype(BF16)
        v = va_ref[0, 0, pl.ds(start, tk), :]
        acc_ref[...] = alpha * acc_ref[...] + jnp.dot(p, v, preferred_element_type=F32)
        m_ref[...] = m_new
        return carry

    lax.fori_loop(0, trips, body, 0)
    acc = acc_ref[...]
    o = acc[:, :HEAD_DIM] / acc[:, HEAD_DIM:HEAD_DIM + 1]
    out = jnp.concatenate([o[:tq], o[tq:]], axis=1)
    o_ref[0] = (out * gate_ref[0].astype(F32)).astype(o_ref.dtype)


def _global_attention(g, kt, va, *, ctx_len):
    bsz, t_len, _ = g.shape
    tq, tk = ATTN_TQ, ATTN_TK
    return pl.pallas_call(
        functools.partial(_gattn_kernel, tk=tk, n_kt=t_len // tk,
                          ctx_qtiles=ctx_len // tq, ctx_kt=ctx_len // tk),
        out_shape=jax.ShapeDtypeStruct((bsz, t_len, _GW), BF16),
        grid=(bsz, N_KV, t_len // tq),
        in_specs=[
            pl.BlockSpec((1, tq, 128), lambda b, gi, qi: (b, qi, gi)),
            pl.BlockSpec((1, 1, HEAD_DIM, t_len), lambda b, gi, qi: (b, gi, 0, 0)),
            pl.BlockSpec((1, 1, t_len, 128), lambda b, gi, qi: (b, gi, 0, 0)),
            pl.BlockSpec((1, tq, 128), lambda b, gi, qi: (b, qi, 4 + gi)),
        ],
        out_specs=pl.BlockSpec((1, tq, 128), lambda b, gi, qi: (b, qi, gi)),
        scratch_shapes=[pltpu.VMEM((2 * tq, 1), F32), pltpu.VMEM((2 * tq, 128), F32)],
        compiler_params=pltpu.CompilerParams(
            dimension_semantics=("arbitrary", "arbitrary", "arbitrary"),
            vmem_limit_bytes=VMEM_LIMIT),
        name="global_attention",
    )(g, kt, va, g)


def _kv_layouts(g):
    bsz, t_len, _ = g.shape
    k = g[:, :, _GW:_GW + KV_WIDTH].reshape(bsz, t_len, N_KV, HEAD_DIM)
    v = g[:, :, _GW + KV_WIDTH:_GW + 2 * KV_WIDTH].reshape(bsz, t_len, N_KV, HEAD_DIM)
    kt = jnp.transpose(k, (0, 2, 3, 1))
    vt = jnp.transpose(v, (0, 2, 1, 3))
    ones = jnp.ones((bsz, N_KV, t_len, 1), v.dtype)
    zeros = jnp.zeros((bsz, N_KV, t_len, 128 - HEAD_DIM - 1), v.dtype)
    return kt, jnp.concatenate([vt, ones, zeros], axis=-1)


def _wattn_kernel(sink_ref, q_ref, kp_ref, kc_ref, kn_ref, vp_ref, vc_ref, vn_ref,
                  kx_ref, vx_ref, gate_ref, o_ref, *, ctx_chunks, n_chunks):
    c = pl.program_id(1)
    q = q_ref[0]
    k_loc = jnp.concatenate([kp_ref[0], kc_ref[0], kn_ref[0]], axis=0)
    v_loc = jnp.concatenate([vp_ref[0], vc_ref[0], vn_ref[0]], axis=0)
    k_ctx = kx_ref[0]
    v_ctx = vx_ref[0]
    qi = lax.broadcasted_iota(jnp.int32, (CHUNK, 3 * CHUNK), 0)
    kj = lax.broadcasted_iota(jnp.int32, (CHUNK, 3 * CHUNK), 1)
    kblk = c - 1 + kj // CHUNK
    dist = jnp.abs(qi + CHUNK - kj)
    valid = ((dist <= WINDOW) & (kblk >= ctx_chunks) & (kblk < n_chunks)
             & (c >= ctx_chunks))
    outs = []
    for hh in range(N_HEADS):
        gi = hh // (N_HEADS // N_KV)
        qh = q[:, hh * HEAD_DIM:(hh + 1) * HEAD_DIM]
        kh = k_loc[:, gi * HEAD_DIM:(gi + 1) * HEAD_DIM]
        vh = v_loc[:, gi * HEAD_DIM:(gi + 1) * HEAD_DIM]
        kxh = k_ctx[:, gi * HEAD_DIM:(gi + 1) * HEAD_DIM]
        vxh = v_ctx[:, gi * HEAD_DIM:(gi + 1) * HEAD_DIM]
        nt = (((1,), (1,)), ((), ()))
        s_loc = lax.dot_general(qh, kh, nt, preferred_element_type=F32)
        s_loc = jnp.where(valid, s_loc, NEG_INF)
        s_ctx = lax.dot_general(qh, kxh, nt, preferred_element_type=F32)
        sink = sink_ref[hh]
        m = jnp.maximum(jnp.maximum(jnp.max(s_loc, axis=-1, keepdims=True),
                                    jnp.max(s_ctx, axis=-1, keepdims=True)), sink)
        p_loc = jnp.exp(s_loc - m)
        p_ctx = jnp.exp(s_ctx - m)
        denom = (jnp.sum(p_loc, axis=-1, keepdims=True) + jnp.sum(p_ctx, axis=-1, keepdims=True)
                 + jnp.exp(sink - m))
        o = (jnp.dot(p_loc.astype(BF16), vh, preferred_element_type=F32)
             + jnp.dot(p_ctx.astype(BF16), vxh, preferred_element_type=F32))
        outs.append(o / denom)
    out = jnp.concatenate(outs, axis=1)
    o_ref[0] = (out * gate_ref[0].astype(F32)).astype(o_ref.dtype)


def _window_attention(s, sink, *, ctx_len):
    bsz, t_len, _ = s.shape
    nc = t_len // CHUNK
    cc = ctx_len // CHUNK
    kcol = _GW // KV_WIDTH
    vcol = kcol + 1

    def nb(col, off):
        return pl.BlockSpec((1, CHUNK, KV_WIDTH),
                            lambda b, c: (b, jnp.clip(c + off, 0, nc - 1), col))

    return pl.pallas_call(
        functools.partial(_wattn_kernel, ctx_chunks=cc, n_chunks=nc),
        out_shape=jax.ShapeDtypeStruct((bsz, t_len, _GW), BF16),
        grid=(bsz, nc),
        in_specs=[
            pl.BlockSpec(memory_space=pltpu.SMEM),
            pl.BlockSpec((1, CHUNK, _GW), lambda b, c: (b, c, 0)),
            nb(kcol, -1), nb(kcol, 0), nb(kcol, 1),
            nb(vcol, -1), nb(vcol, 0), nb(vcol, 1),
            pl.BlockSpec((1, ctx_len, KV_WIDTH), lambda b, c: (b, 0, kcol)),
            pl.BlockSpec((1, ctx_len, KV_WIDTH), lambda b, c: (b, 0, vcol)),
            pl.BlockSpec((1, CHUNK, _GW), lambda b, c: (b, c, 2)),
        ],
        out_specs=pl.BlockSpec((1, CHUNK, _GW), lambda b, c: (b, c, 0)),
        compiler_params=pltpu.CompilerParams(
            dimension_semantics=("arbitrary", "arbitrary"), vmem_limit_bytes=VMEM_LIMIT),
        name="window_attention",
    )(sink, s, s, s, s, s, s, s, s, s, s)


def _outproj_kernel(x_ref, mod_ref, mlp_ref, yf_ref, yb_ref, rg_ref, ga_ref, sw_ref,
                    rn_ref, bd_ref, w_ref, o_ref, *, ctx_tiles, d_model):
    b = pl.program_id(0)
    t = pl.program_id(1)
    row = jnp.where(t < ctx_tiles, 2, b)
    gate = mod_ref[pl.ds(row, 1), :][:, 2 * d_model:]
    y = yf_ref[0] + yb_ref[0]
    bd = bd_ref[...]
    ms = jnp.concatenate([_group_mean_sq(y[:, :128], bd), _group_mean_sq(y[:, 128:], bd)], axis=1)
    ret = (y * lax.rsqrt(ms + RMS_EPS) * rn_ref[...]) * rg_ref[0]
    acc = jnp.dot(mlp_ref[0], w_ref[0:_GW, :], preferred_element_type=F32)
    acc += jnp.dot(ret.astype(BF16), w_ref[_GW:2 * _GW, :], preferred_element_type=F32)
    acc += jnp.dot(ga_ref[0], w_ref[2 * _GW:3 * _GW, :], preferred_element_type=F32)
    acc += jnp.dot(sw_ref[0], w_ref[3 * _GW:4 * _GW, :], preferred_element_type=F32)
    o_ref[0] = x_ref[0] + gate * acc


def _outproj(xs, mod, mlp, yf, yb, r, ga, sw, rn, bd, w_out, *, ctx_len):
    bsz, t_len, d = xs.shape
    tm = TOKEN_TILE
    const2 = lambda b, t: (0, 0)
    tok = lambda b, t: (b, t, 0)
    grp = pl.BlockSpec((1, tm, _GW), tok)
    return pl.pallas_call(
        functools.partial(_outproj_kernel, ctx_tiles=ctx_len // tm, d_model=d),
        out_shape=jax.ShapeDtypeStruct(xs.shape, xs.dtype),
        grid=(bsz, t_len // tm),
        in_specs=[
            pl.BlockSpec((1, tm, d), tok),
            pl.BlockSpec((8, 3 * d), const2),
            grp, grp, grp,
            pl.BlockSpec((1, tm, _GW), lambda b, t: (b, t, 3)),
            grp, grp,
            pl.BlockSpec((1, _GW), const2),
            pl.BlockSpec((128, 128), const2),
            pl.BlockSpec((4 * _GW, d), const2),
        ],
        out_specs=pl.BlockSpec((1, tm, d), tok),
        input_output_aliases={0: 0},
        compiler_params=pltpu.CompilerParams(
            dimension_semantics=("arbitrary", "arbitrary"), vmem_limit_bytes=VMEM_LIMIT),
        name="outproj",
    )(xs, mod, mlp, yf, yb, r, ga, sw, rn, bd, w_out)


def _rope_tables(seq_len, ctx_len):
    rows = seq_len // GRID_W
    row = jnp.broadcast_to(jnp.arange(rows, dtype=F32)[:, None], (rows, GRID_W)).reshape(-1)
    col = jnp.broadcast_to(jnp.arange(GRID_W, dtype=F32)[None, :], (rows, GRID_W)).reshape(-1)
    half = HEAD_DIM // 2
    inv_freq = 1.0 / (ROPE_BASE ** (jnp.arange(0, half, 2, dtype=F32) / half))
    ang_r = row[:, None] * inv_freq[None, :]
    ang_c = col[:, None] * inv_freq[None, :]
    ang = jnp.concatenate([ang_r, ang_r, ang_c, ang_c], axis=-1)
    cos = jnp.concatenate([jnp.ones((ctx_len, HEAD_DIM), F32), jnp.cos(ang)], axis=0)
    sin = jnp.concatenate([jnp.zeros((ctx_len, HEAD_DIM), F32), jnp.sin(ang)], axis=0)
    first = (jnp.arange(HEAD_DIM) % (half)) < (half // 2)
    sina = jnp.where(first[None, :], -sin, 0.0)
    sinb = jnp.where(first[None, :], 0.0, sin)
    two = lambda a: jnp.concatenate([a, a], axis=-1)
    return two(cos), two(sina), two(sinb)


def kernel(x, c, ctx, c_ctx, norm_gain, w_mod, b_mod, w_in, w_out, mlp_mix, mlp_bias,
           ret_decay_fwd, ret_decay_bwd, ret_norm, attn_q_norm, attn_k_norm,
           swa_q_norm, swa_k_norm, swa_sink):
    bsz, seq_len, d = x.shape
    ctx_len = ctx.shape[1]
    depth = w_in.shape[0]
    assert w_in.shape[2] == IN_WIDTH and w_out.shape[1] == 4 * _GW
    assert ctx_len % TOKEN_TILE == 0 and seq_len % TOKEN_TILE == 0 and bsz <= 2

    cond = jnp.zeros((8, d), F32).at[:bsz].set(c).at[2].set(c_ctx)
    mod_all = _modulation(cond, w_mod, b_mod)

    cos, sina, sinb = _rope_tables(seq_len, ctx_len)
    lane_group = jnp.arange(128) // HEAD_DIM
    bd = (lane_group[:, None] == lane_group[None, :]).astype(BF16) * (1.0 / HEAD_DIM)
    bd = bd.astype(BF16)
    w_in_b = w_in.astype(BF16)
    w_out_b = w_out.astype(BF16)
    mix_b = mlp_mix.astype(BF16)
    two = lambda a: jnp.concatenate([a, a], axis=-1)

    xs = jnp.concatenate([ctx, x], axis=1)
    for i in range(depth):
        mod = mod_all[i]
        mbias = jnp.repeat(mlp_bias[i].T, HEAD_DIM, axis=1)
        qkg = jnp.stack([two(attn_q_norm[i]) * ATTN_SCALE, two(attn_k_norm[i]),
                         two(swa_q_norm[i]) * ATTN_SCALE, two(swa_k_norm[i])]).astype(F32)
        mlp, r, g, s = _inproj(xs, mod, norm_gain[i][None, :], w_in_b[i], mix_b[i], mbias, bd,
                               cos, sina, sinb, qkg, ctx_len=ctx_len)
        lg_f = -jnp.exp(ret_decay_fwd[i].astype(F32))
        lg_b = -jnp.exp(ret_decay_bwd[i].astype(F32))
        decf, decb, vecs = _retention_tables(lg_f, lg_b)
        yf, yb = _retention(r, decf, decb, vecs, ctx_len=ctx_len)
        kt, va = _kv_layouts(g)
        ga = _global_attention(g, kt, va, ctx_len=ctx_len)
        sw = _window_attention(s, swa_sink[i].astype(F32), ctx_len=ctx_len)
        xs = _outproj(xs, mod, mlp, yf, yb, r, ga, sw, ret_norm[i].reshape(1, _GW), bd,
                      w_out_b[i], ctx_len=ctx_len)
    return xs[:, ctx_len:, :]
```

```python
import functools

import numpy as np
import jax
import jax.numpy as jnp
from jax import lax
from jax.experimental import pallas as pl
from jax.experimental.pallas import tpu as pltpu

F32 = jnp.float32
BF16 = jnp.bfloat16

GRID_W = 64
CHUNK = 128
WINDOW = 128
HEAD_DIM = 64
N_HEADS = 4
GROUP_WIDTH = N_HEADS * HEAD_DIM
N_KV = 2
KV_WIDTH = N_KV * HEAD_DIM
ROPE_BASE = 10000.0
RMS_EPS = 1e-6
ATTN_SCALE = HEAD_DIM ** -0.5
NEG_INF = -1e30

_GW = GROUP_WIDTH
C_AUV, C_AG = 0, 2 * _GW
C_RQ, C_RK, C_RV, C_RG = 3 * _GW, 4 * _GW, 5 * _GW, 6 * _GW
C_GQ = 7 * _GW
C_GK = C_GQ + _GW
C_GV = C_GK + KV_WIDTH
C_GG = C_GV + KV_WIDTH
C_SQ = C_GG + _GW
C_SK = C_SQ + _GW
C_SV = C_SK + KV_WIDTH
C_SG = C_SV + KV_WIDTH
IN_WIDTH = C_SG + _GW
QKV_WIDTH = 2 * _GW + 2 * KV_WIDTH

TOKEN_TILE = 256
ATTN_TQ = 256
ATTN_TK_ONLINE = 256
ATTN_TK_FAST = (3328, 256)
MAX_UNSHIFTED_SCORE = 32.0
VMEM_LIMIT = 48 * 1024 * 1024


def _silu(x):
    return x / (1.0 + jnp.exp(-x))


def _head_mask(width, head):
    lane = lax.broadcasted_iota(jnp.int32, (1, width), 1)
    return (lane // HEAD_DIM) == head


def _group_mean_sq(y, bd):
    sq = y * y
    hi = sq.astype(BF16)
    lo = (sq - hi.astype(F32)).astype(BF16)
    return (jnp.dot(hi, bd, preferred_element_type=F32)
            + jnp.dot(lo, bd, preferred_element_type=F32))


def _mod_kernel(cond_ref, w_ref, b_ref, o_ref):
    a = _silu(cond_ref[...])
    o_ref[0] = jnp.dot(a, w_ref[0], preferred_element_type=F32) + b_ref[0]


def _modulation(cond, w_mod, b_mod):
    depth, d, d3 = w_mod.shape
    nblk = d3 // d
    return pl.pallas_call(
        _mod_kernel,
        out_shape=jax.ShapeDtypeStruct((depth, 8, d3), F32),
        grid=(depth, nblk),
        in_specs=[
            pl.BlockSpec((8, d), lambda i, j: (0, 0)),
            pl.BlockSpec((1, d, d), lambda i, j: (i, 0, j)),
            pl.BlockSpec((1, 1, d), lambda i, j: (i, 0, j)),
        ],
        out_specs=pl.BlockSpec((1, 8, d), lambda i, j: (i, 0, j)),
        compiler_params=pltpu.CompilerParams(
            dimension_semantics=("arbitrary", "arbitrary"), vmem_limit_bytes=VMEM_LIMIT),
        name="modulation",
    )(cond, w_mod, b_mod.reshape(depth, 1, d3))


def _inproj_kernel(x_ref, mod_ref, gain_ref, w_ref, mix_ref, mbias_ref, bd_ref,
                   cos_ref, sina_ref, sinb_ref, qkg_ref,
                   mlp_ref, r_ref, g_ref, s_ref, *, ctx_tiles, d_model):
    b = pl.program_id(0)
    t = pl.program_id(1)
    d = d_model
    row = jnp.where(t < ctx_tiles, 2, b)
    mod = mod_ref[pl.ds(row, 1), :]
    shift = mod[:, :d]
    scale = mod[:, d:2 * d]
    x = x_ref[0]
    ms = jnp.mean(x * x, axis=-1, keepdims=True)
    h = x * lax.rsqrt(ms + RMS_EPS) * gain_ref[...]
    h = (h * (1.0 + scale) + shift).astype(BF16)

    def proj(lo, width):
        return jnp.dot(h, w_ref[:, lo:lo + width], preferred_element_type=F32)

    bd = bd_ref[...]
    cos = cos_ref[...]
    sina = sina_ref[...]
    sinb = sinb_ref[...]

    def norm_rope(y, gain_row):
        yn = y * lax.rsqrt(_group_mean_sq(y, bd) + RMS_EPS) * gain_row
        return (yn * cos + pltpu.roll(yn, 128 - 16, 1) * sina + pltpu.roll(yn, 16, 1) * sinb)

    uv = jax.nn.gelu(proj(C_AUV, 2 * _GW), approximate=True)
    u = uv[:, :_GW]
    v = uv[:, _GW:].astype(BF16)
    ag = _silu(proj(C_AG, _GW))
    tm = x.shape[0]
    masks = [_head_mask(_GW, hh) for hh in range(N_HEADS)]
    for c in range(tm // CHUNK):
        rows = slice(c * CHUNK, (c + 1) * CHUNK)
        vc = v[rows]
        sv = mbias_ref[...]
        for hh in range(N_HEADS):
            mixed = jnp.dot(mix_ref[hh], vc, preferred_element_type=F32)
            sv = sv + jnp.where(masks[hh], mixed, 0.0)
        mlp_ref[0, rows, :] = (u[rows] * sv * ag[rows]).astype(mlp_ref.dtype)

    r_ref[0, :, 0:_GW] = proj(C_RQ, _GW)
    r_ref[0, :, _GW:2 * _GW] = proj(C_RK, _GW) * ATTN_SCALE
    r_ref[0, :, 2 * _GW:3 * _GW] = proj(C_RV, _GW)
    r_ref[0, :, 3 * _GW:4 * _GW] = _silu(proj(C_RG, _GW))

    for out_ref, cq, ck, cv, cg, gi in ((g_ref, C_GQ, C_GK, C_GV, C_GG, 0),
                                        (s_ref, C_SQ, C_SK, C_SV, C_SG, 2)):
        qg = qkg_ref[gi:gi + 1, :]
        kg = qkg_ref[gi + 1:gi + 2, :]
        q = proj(cq, _GW)
        out_ref[0, :, 0:128] = norm_rope(q[:, :128], qg).astype(out_ref.dtype)
        out_ref[0, :, 128:256] = norm_rope(q[:, 128:], qg).astype(out_ref.dtype)
        out_ref[0, :, 256:384] = norm_rope(proj(ck, KV_WIDTH), kg).astype(out_ref.dtype)
        out_ref[0, :, 384:512] = proj(cv, KV_WIDTH).astype(out_ref.dtype)
        out_ref[0, :, 512:768] = _silu(proj(cg, _GW)).astype(out_ref.dtype)


def _inproj(xs, mod, gain, w_in, mix, mbias, bd, cos, sina, sinb, qkg, *, ctx_len):
    bsz, t_len, d = xs.shape
    tm = TOKEN_TILE
    nt = t_len // tm
    const2 = lambda b, t: (0, 0)
    tok = lambda b, t: (b, t, 0)
    tab = lambda b, t: (t, 0)
    return pl.pallas_call(
        functools.partial(_inproj_kernel, ctx_tiles=ctx_len // tm, d_model=d),
        out_shape=(
            jax.ShapeDtypeStruct((bsz, t_len, _GW), BF16),
            jax.ShapeDtypeStruct((bsz, t_len, 4 * _GW), F32),
            jax.ShapeDtypeStruct((bsz, t_len, QKV_WIDTH), BF16),
            jax.ShapeDtypeStruct((bsz, t_len, QKV_WIDTH), BF16),
        ),
        grid=(bsz, nt),
        in_specs=[
            pl.BlockSpec((1, tm, d), tok),
            pl.BlockSpec((8, 3 * d), const2),
            pl.BlockSpec((1, d), const2),
            pl.BlockSpec((d, IN_WIDTH), const2),
            pl.BlockSpec((N_HEADS, CHUNK, CHUNK), lambda b, t: (0, 0, 0)),
            pl.BlockSpec((CHUNK, _GW), const2),
            pl.BlockSpec((128, 128), const2),
            pl.BlockSpec((tm, 128), tab),
            pl.BlockSpec((tm, 128), tab),
            pl.BlockSpec((tm, 128), tab),
            pl.BlockSpec((4, 128), const2),
        ],
        out_specs=(
            pl.BlockSpec((1, tm, _GW), tok),
            pl.BlockSpec((1, tm, 4 * _GW), tok),
            pl.BlockSpec((1, tm, QKV_WIDTH), tok),
            pl.BlockSpec((1, tm, QKV_WIDTH), tok),
        ),
        compiler_params=pltpu.CompilerParams(
            dimension_semantics=("arbitrary", "arbitrary"), vmem_limit_bytes=VMEM_LIMIT),
        name="inproj",
    )(xs, mod, gain, w_in, mix, mbias, bd, cos, sina, sinb, qkg)


def _ret_direction(q, k, v, dec_ref, qdec, kdec, cdec, state_ref, y_ref):
    qb = q.astype(BF16)
    kb = k.astype(BF16)
    vb = v.astype(BF16)
    state = state_ref[...]
    y = jnp.dot(qb, state.astype(BF16), preferred_element_type=F32) * qdec
    for hh in range(N_HEADS):
        mask = _head_mask(_GW, hh)
        qh = jnp.where(mask, q, 0.0).astype(BF16)
        sc = lax.dot_general(qh, kb, (((1,), (1,)), ((), ())), preferred_element_type=F32)
        sc = (sc * dec_ref[hh]).astype(BF16)
        y = y + jnp.where(mask, jnp.dot(sc, vb, preferred_element_type=F32), 0.0)
    y_ref[0] = y
    kd = (k * kdec).astype(BF16)
    kv = lax.dot_general(kd, vb, (((0,), (0,)), ((), ())), preferred_element_type=F32)
    row_head = lax.broadcasted_iota(jnp.int32, (_GW, _GW), 0) // HEAD_DIM
    col_head = lax.broadcasted_iota(jnp.int32, (_GW, _GW), 1) // HEAD_DIM
    state_ref[...] = state * cdec + jnp.where(row_head == col_head, kv, 0.0)


def _ret_kernel(qf_ref, kf_ref, vf_ref, qb_ref, kb_ref, vb_ref,
                decf_ref, decb_ref, vec_ref, yf_ref, yb_ref, sf_ref, sb_ref):
    @pl.when(pl.program_id(1) == 0)
    def _():
        sf_ref[...] = jnp.zeros_like(sf_ref)
        sb_ref[...] = jnp.zeros_like(sb_ref)

    _ret_direction(qf_ref[0], kf_ref[0], vf_ref[0], decf_ref,
                   vec_ref[0], vec_ref[1], vec_ref[2, 0:1, :], sf_ref, yf_ref)
    _ret_direction(qb_ref[0], kb_ref[0], vb_ref[0], decb_ref,
                   vec_ref[3], vec_ref[4], vec_ref[5, 0:1, :], sb_ref, yb_ref)


def _retention(r, decf, decb, vecs, *, ctx_len):
    bsz, t_len, _ = r.shape
    nc = t_len // CHUNK
    cc = ctx_len // CHUNK

    def bwd_chunk(t):
        return jnp.where(t < cc, cc - 1 - t, nc - 1 + cc - t)

    def fwd(col):
        return pl.BlockSpec((1, CHUNK, _GW), lambda b, t: (b, t, col))

    def bwd(col):
        return pl.BlockSpec((1, CHUNK, _GW), lambda b, t: (b, bwd_chunk(t), col))

    return pl.pallas_call(
        _ret_kernel,
        out_shape=(jax.ShapeDtypeStruct((bsz, t_len, _GW), F32),
                   jax.ShapeDtypeStruct((bsz, t_len, _GW), F32)),
        grid=(bsz, nc),
        in_specs=[fwd(0), fwd(1), fwd(2), bwd(0), bwd(1), bwd(2),
                  pl.BlockSpec((N_HEADS, CHUNK, CHUNK), lambda b, t: (0, 0, 0)),
                  pl.BlockSpec((N_HEADS, CHUNK, CHUNK), lambda b, t: (0, 0, 0)),
                  pl.BlockSpec((6, CHUNK, _GW), lambda b, t: (0, 0, 0))],
        out_specs=(pl.BlockSpec((1, CHUNK, _GW), lambda b, t: (b, t, 0)),
                   pl.BlockSpec((1, CHUNK, _GW), lambda b, t: (b, bwd_chunk(t), 0))),
        scratch_shapes=[pltpu.VMEM((_GW, _GW), F32), pltpu.VMEM((_GW, _GW), F32)],
        compiler_params=pltpu.CompilerParams(
            dimension_semantics=("arbitrary", "arbitrary"), vmem_limit_bytes=VMEM_LIMIT),
        name="retention",
    )(r, r, r, r, r, r, decf, decb, vecs)


def _retention_tables(lg_f, lg_b):
    pos = jnp.arange(CHUNK, dtype=F32)
    diff = pos[:, None] - pos[None, :]
    keep_f = diff >= 0
    dec_f = jnp.where(keep_f, jnp.exp(lg_f[:, None, None] * jnp.where(keep_f, diff, 0.0)), 0.0)
    keep_b = diff < 0
    dec_b = jnp.where(keep_b, jnp.exp(lg_b[:, None, None] * jnp.where(keep_b, -diff, 0.0)), 0.0)

    def lanes(tab):
        return jnp.repeat(tab.T, HEAD_DIM, axis=1)

    qdec_f = lanes(jnp.exp(lg_f[:, None] * (pos + 1.0)))
    kdec_f = lanes(jnp.exp(lg_f[:, None] * (CHUNK - 1.0 - pos)))
    qdec_b = lanes(jnp.exp(lg_b[:, None] * (CHUNK - pos)))
    kdec_b = lanes(jnp.exp(lg_b[:, None] * pos))
    cdec_f = jnp.broadcast_to(jnp.repeat(jnp.exp(lg_f * CHUNK), HEAD_DIM)[None, :], (CHUNK, _GW))
    cdec_b = jnp.broadcast_to(jnp.repeat(jnp.exp(lg_b * CHUNK), HEAD_DIM)[None, :], (CHUNK, _GW))
    vecs = jnp.stack([qdec_f, kdec_f, cdec_f, qdec_b, kdec_b, cdec_b]).astype(F32)
    return dec_f.astype(F32), dec_b.astype(F32), vecs


def _gattn_kernel(q_ref, kt_ref, va_ref, gate_ref, o_ref, m_ref, acc_ref, *,
                  tk, n_kt, ctx_qtiles, ctx_len, online):
    qi = pl.program_id(2)
    tq = q_ref.shape[1]
    q = q_ref[0]
    q2 = jnp.concatenate([q[:, :HEAD_DIM], q[:, HEAD_DIM:]], axis=0)

    def tile(start, width):
        k = kt_ref[0, 0, :, pl.ds(start, width)]
        v = va_ref[0, 0, pl.ds(start, width), :]
        s = jnp.dot(q2, k, preferred_element_type=F32)
        if not online:
            acc_ref[...] += jnp.dot(jnp.exp(s).astype(BF16), v, preferred_element_type=F32)
            return
        m_prev = m_ref[...]
        m_new = jnp.maximum(m_prev, jnp.max(s, axis=-1, keepdims=True))
        p = jnp.exp(s - m_new).astype(BF16)
        acc_ref[...] = (jnp.exp(m_prev - m_new) * acc_ref[...]
                        + jnp.dot(p, v, preferred_element_type=F32))
        m_ref[...] = m_new

    m_ref[...] = jnp.full(m_ref.shape, NEG_INF, F32)
    acc_ref[...] = jnp.zeros_like(acc_ref)

    @pl.when(qi < ctx_qtiles)
    def _():
        tile(0, ctx_len)

    @pl.when(qi >= ctx_qtiles)
    def _():
        def body(kt, carry):
            tile(pl.multiple_of(kt * tk, tk), tk)
            return carry
        lax.fori_loop(0, n_kt, body, 0)

    acc = acc_ref[...]
    o = acc[:, :HEAD_DIM] / acc[:, HEAD_DIM:HEAD_DIM + 1]
    out = jnp.concatenate([o[:tq], o[tq:]], axis=1)
    o_ref[0] = (out * gate_ref[0].astype(F32)).astype(o_ref.dtype)


def _global_attention_call(g, kt, va, *, ctx_len, online):
    bsz, t_len, _ = g.shape
    tq = ATTN_TQ
    tk = ATTN_TK_ONLINE if online else next(w for w in ATTN_TK_FAST if t_len % w == 0)
    return pl.pallas_call(
        functools.partial(_gattn_kernel, tk=tk, n_kt=t_len // tk,
                          ctx_qtiles=ctx_len // tq, ctx_len=ctx_len, online=online),
        out_shape=jax.ShapeDtypeStruct((bsz, t_len, _GW), BF16),
        grid=(bsz, N_KV, t_len // tq),
        in_specs=[
            pl.BlockSpec((1, tq, 128), lambda b, gi, qi: (b, qi, gi)),
            pl.BlockSpec((1, 1, HEAD_DIM, t_len), lambda b, gi, qi: (b, gi, 0, 0)),
            pl.BlockSpec((1, 1, t_len, 128), lambda b, gi, qi: (b, gi, 0, 0)),
            pl.BlockSpec((1, tq, 128), lambda b, gi, qi: (b, qi, 4 + gi)),
        ],
        out_specs=pl.BlockSpec((1, tq, 128), lambda b, gi, qi: (b, qi, gi)),
        scratch_shapes=[pltpu.VMEM((2 * tq, 1), F32), pltpu.VMEM((2 * tq, 128), F32)],
        compiler_params=pltpu.CompilerParams(
            dimension_semantics=("arbitrary", "arbitrary", "arbitrary"),
            vmem_limit_bytes=VMEM_LIMIT),
        name="global_attention_online" if online else "global_attention",
    )(g, kt, va, g)


def _global_attention(g, kt, va, score_bound, *, ctx_len):
    return lax.cond(
        score_bound <= MAX_UNSHIFTED_SCORE,
        functools.partial(_global_attention_call, ctx_len=ctx_len, online=False),
        functools.partial(_global_attention_call, ctx_len=ctx_len, online=True),
        g, kt, va)


def _kv_layouts(g):
    bsz, t_len, _ = g.shape
    k = g[:, :, _GW:_GW + KV_WIDTH].reshape(bsz, t_len, N_KV, HEAD_DIM)
    v = g[:, :, _GW + KV_WIDTH:_GW + 2 * KV_WIDTH].reshape(bsz, t_len, N_KV, HEAD_DIM)
    kt = jnp.transpose(k, (0, 2, 3, 1))
    vt = jnp.transpose(v, (0, 2, 1, 3))
    ones = jnp.ones((bsz, N_KV, t_len, 1), v.dtype)
    zeros = jnp.zeros((bsz, N_KV, t_len, 128 - HEAD_DIM - 1), v.dtype)
    return kt, jnp.concatenate([vt, ones, zeros], axis=-1)


def _wattn_kernel(sink_ref, q_ref, kp_ref, kc_ref, kn_ref, vp_ref, vc_ref, vn_ref,
                  kx_ref, vx_ref, gate_ref, o_ref, *, ctx_chunks, n_chunks):
    c = pl.program_id(1)
    q = q_ref[0]
    k_loc = jnp.concatenate([kp_ref[0], kc_ref[0], kn_ref[0]], axis=0)
    v_loc = jnp.concatenate([vp_ref[0], vc_ref[0], vn_ref[0]], axis=0)
    k_ctx = kx_ref[0]
    v_ctx = vx_ref[0]
    qi = lax.broadcasted_iota(jnp.int32, (CHUNK, 3 * CHUNK), 0)
    kj = lax.broadcasted_iota(jnp.int32, (CHUNK, 3 * CHUNK), 1)
    kblk = c - 1 + kj // CHUNK
    dist = jnp.abs(qi + CHUNK - kj)
    valid = ((dist <= WINDOW) & (kblk >= ctx_chunks) & (kblk < n_chunks)
             & (c >= ctx_chunks))
    outs = []
    for hh in range(N_HEADS):
        gi = hh // (N_HEADS // N_KV)
        qh = q[:, hh * HEAD_DIM:(hh + 1) * HEAD_DIM]
        kh = k_loc[:, gi * HEAD_DIM:(gi + 1) * HEAD_DIM]
        vh = v_loc[:, gi * HEAD_DIM:(gi + 1) * HEAD_DIM]
        kxh = k_ctx[:, gi * HEAD_DIM:(gi + 1) * HEAD_DIM]
        vxh = v_ctx[:, gi * HEAD_DIM:(gi + 1) * HEAD_DIM]
        nt = (((1,), (1,)), ((), ()))
        s_loc = lax.dot_general(qh, kh, nt, preferred_element_type=F32)
        s_loc = jnp.where(valid, s_loc, NEG_INF)
        s_ctx = lax.dot_general(qh, kxh, nt, preferred_element_type=F32)
        sink = sink_ref[hh]
        m = jnp.maximum(jnp.maximum(jnp.max(s_loc, axis=-1, keepdims=True),
                                    jnp.max(s_ctx, axis=-1, keepdims=True)), sink)
        p_loc = jnp.exp(s_loc - m)
        p_ctx = jnp.exp(s_ctx - m)
        denom = (jnp.sum(p_loc, axis=-1, keepdims=True) + jnp.sum(p_ctx, axis=-1, keepdims=True)
                 + jnp.exp(sink - m))
        o = (jnp.dot(p_loc.astype(BF16), vh, preferred_element_type=F32)
             + jnp.dot(p_ctx.astype(BF16), vxh, preferred_element_type=F32))
        outs.append(o / denom)
    out = jnp.concatenate(outs, axis=1)
    o_ref[0] = (out * gate_ref[0].astype(F32)).astype(o_ref.dtype)


def _window_attention(s, sink, *, ctx_len):
    bsz, t_len, _ = s.shape
    nc = t_len // CHUNK
    cc = ctx_len // CHUNK
    kcol = _GW // KV_WIDTH
    vcol = kcol + 1

    def nb(col, off):
        return pl.BlockSpec((1, CHUNK, KV_WIDTH),
                            lambda b, c: (b, jnp.clip(c + off, 0, nc - 1), col))

    return pl.pallas_call(
        functools.partial(_wattn_kernel, ctx_chunks=cc, n_chunks=nc),
        out_shape=jax.ShapeDtypeStruct((bsz, t_len, _GW), BF16),
        grid=(bsz, nc),
        in_specs=[
            pl.BlockSpec(memory_space=pltpu.SMEM),
            pl.BlockSpec((1, CHUNK, _GW), lambda b, c: (b, c, 0)),
            nb(kcol, -1), nb(kcol, 0), nb(kcol, 1),
            nb(vcol, -1), nb(vcol, 0), nb(vcol, 1),
            pl.BlockSpec((1, ctx_len, KV_WIDTH), lambda b, c: (b, 0, kcol)),
            pl.BlockSpec((1, ctx_len, KV_WIDTH), lambda b, c: (b, 0, vcol)),
            pl.BlockSpec((1, CHUNK, _GW), lambda b, c: (b, c, 2)),
        ],
        out_specs=pl.BlockSpec((1, CHUNK, _GW), lambda b, c: (b, c, 0)),
        compiler_params=pltpu.CompilerParams(
            dimension_semantics=("arbitrary", "arbitrary"), vmem_limit_bytes=VMEM_LIMIT),
        name="window_attention",
    )(sink, s, s, s, s, s, s, s, s, s, s)


def _outproj_kernel(x_ref, mod_ref, mlp_ref, yf_ref, yb_ref, rg_ref, ga_ref, sw_ref,
                    rn_ref, bd_ref, w_ref, o_ref, *, ctx_tiles, d_model):
    b = pl.program_id(0)
    t = pl.program_id(1)
    row = jnp.where(t < ctx_tiles, 2, b)
    gate = mod_ref[pl.ds(row, 1), :][:, 2 * d_model:]
    y = yf_ref[0] + yb_ref[0]
    bd = bd_ref[...]
    ms = jnp.concatenate([_group_mean_sq(y[:, :128], bd), _group_mean_sq(y[:, 128:], bd)], axis=1)
    ret = (y * lax.rsqrt(ms + RMS_EPS) * rn_ref[...]) * rg_ref[0]
    acc = jnp.dot(mlp_ref[0], w_ref[0:_GW, :], preferred_element_type=F32)
    acc += jnp.dot(ret.astype(BF16), w_ref[_GW:2 * _GW, :], preferred_element_type=F32)
    acc += jnp.dot(ga_ref[0], w_ref[2 * _GW:3 * _GW, :], preferred_element_type=F32)
    acc += jnp.dot(sw_ref[0], w_ref[3 * _GW:4 * _GW, :], preferred_element_type=F32)
    o_ref[0] = x_ref[0] + gate * acc


def _outproj(xs, mod, mlp, yf, yb, r, ga, sw, rn, bd, w_out, *, ctx_len):
    bsz, t_len, d = xs.shape
    tm = TOKEN_TILE
    const2 = lambda b, t: (0, 0)
    tok = lambda b, t: (b, t, 0)
    grp = pl.BlockSpec((1, tm, _GW), tok)
    return pl.pallas_call(
        functools.partial(_outproj_kernel, ctx_tiles=ctx_len // tm, d_model=d),
        out_shape=jax.ShapeDtypeStruct(xs.shape, xs.dtype),
        grid=(bsz, t_len // tm),
        in_specs=[
            pl.BlockSpec((1, tm, d), tok),
            pl.BlockSpec((8, 3 * d), const2),
            grp, grp, grp,
            pl.BlockSpec((1, tm, _GW), lambda b, t: (b, t, 3)),
            grp, grp,
            pl.BlockSpec((1, _GW), const2),
            pl.BlockSpec((128, 128), const2),
            pl.BlockSpec((4 * _GW, d), const2),
        ],
        out_specs=pl.BlockSpec((1, tm, d), tok),
        input_output_aliases={0: 0},
        compiler_params=pltpu.CompilerParams(
            dimension_semantics=("arbitrary", "arbitrary"), vmem_limit_bytes=VMEM_LIMIT),
        name="outproj",
    )(xs, mod, mlp, yf, yb, r, ga, sw, rn, bd, w_out)


def _rope_tables(seq_len, ctx_len):
    rows = seq_len // GRID_W
    row = jnp.broadcast_to(jnp.arange(rows, dtype=F32)[:, None], (rows, GRID_W)).reshape(-1)
    col = jnp.broadcast_to(jnp.arange(GRID_W, dtype=F32)[None, :], (rows, GRID_W)).reshape(-1)
    half = HEAD_DIM // 2
    inv_freq = 1.0 / (ROPE_BASE ** (jnp.arange(0, half, 2, dtype=F32) / half))
    ang_r = row[:, None] * inv_freq[None, :]
    ang_c = col[:, None] * inv_freq[None, :]
    ang = jnp.concatenate([ang_r, ang_r, ang_c, ang_c], axis=-1)
    cos = jnp.concatenate([jnp.ones((ctx_len, HEAD_DIM), F32), jnp.cos(ang)], axis=0)
    sin = jnp.concatenate([jnp.zeros((ctx_len, HEAD_DIM), F32), jnp.sin(ang)], axis=0)
    first = (jnp.arange(HEAD_DIM) % (half)) < (half // 2)
    sina = jnp.where(first[None, :], -sin, 0.0)
    sinb = jnp.where(first[None, :], 0.0, sin)
    two = lambda a: jnp.concatenate([a, a], axis=-1)
    return two(cos), two(sina), two(sinb)


def kernel(x, c, ctx, c_ctx, norm_gain, w_mod, b_mod, w_in, w_out, mlp_mix, mlp_bias,
           ret_decay_fwd, ret_decay_bwd, ret_norm, attn_q_norm, attn_k_norm,
           swa_q_norm, swa_k_norm, swa_sink):
    bsz, seq_len, d = x.shape
    ctx_len = ctx.shape[1]
    depth = w_in.shape[0]
    assert w_in.shape[2] == IN_WIDTH and w_out.shape[1] == 4 * _GW
    assert ctx_len % TOKEN_TILE == 0 and seq_len % TOKEN_TILE == 0 and bsz <= 2

    cond = jnp.zeros((8, d), F32).at[:bsz].set(c).at[2].set(c_ctx)
    mod_all = _modulation(cond, w_mod, b_mod)

    cos, sina, sinb = _rope_tables(seq_len, ctx_len)
    lane_group = jnp.arange(128) // HEAD_DIM
    bd = (lane_group[:, None] == lane_group[None, :]).astype(BF16) * (1.0 / HEAD_DIM)
    bd = bd.astype(BF16)
    w_in_b = w_in.astype(BF16)
    w_out_b = w_out.astype(BF16)
    mix_b = mlp_mix.astype(BF16)
    two = lambda a: jnp.concatenate([a, a], axis=-1)

    xs = jnp.concatenate([ctx, x], axis=1)
    for i in range(depth):
        mod = mod_all[i]
        mbias = jnp.repeat(mlp_bias[i].T, HEAD_DIM, axis=1)
        qkg = jnp.stack([two(attn_q_norm[i]) * ATTN_SCALE, two(attn_k_norm[i]),
                         two(swa_q_norm[i]) * ATTN_SCALE, two(swa_k_norm[i])]).astype(F32)
        mlp, r, g, s = _inproj(xs, mod, norm_gain[i][None, :], w_in_b[i], mix_b[i], mbias, bd,
                               cos, sina, sinb, qkg, ctx_len=ctx_len)
        lg_f = -jnp.exp(ret_decay_fwd[i].astype(F32))
        lg_b = -jnp.exp(ret_decay_bwd[i].astype(F32))
        decf, decb, vecs = _retention_tables(lg_f, lg_b)
        yf, yb = _retention(r, decf, decb, vecs, ctx_len=ctx_len)
        kt, va = _kv_layouts(g)
        score_bound = (HEAD_DIM * ATTN_SCALE * jnp.max(jnp.abs(attn_q_norm[i]))
                       * jnp.max(jnp.abs(attn_k_norm[i])))
        ga = _global_attention(g, kt, va, score_bound, ctx_len=ctx_len)
        sw = _window_attention(s, swa_sink[i].astype(F32), ctx_len=ctx_len)
        xs = _outproj(xs, mod, mlp, yf, yb, r, ga, sw, ret_norm[i].reshape(1, _GW), bd,
                      w_out_b[i], ctx_len=ctx_len)
    return xs[:, ctx_len:, :]
```

```python
import functools

import numpy as np
import jax
import jax.numpy as jnp
from jax import lax
from jax.experimental import pallas as pl
from jax.experimental.pallas import tpu as pltpu

F32 = jnp.float32
BF16 = jnp.bfloat16

GRID_W = 64
CHUNK = 128
WINDOW = 128
HEAD_DIM = 64
N_HEADS = 4
GROUP_WIDTH = N_HEADS * HEAD_DIM
N_KV = 2
KV_WIDTH = N_KV * HEAD_DIM
ROPE_BASE = 10000.0
RMS_EPS = 1e-6
ATTN_SCALE = HEAD_DIM ** -0.5
NEG_INF = -1e30

_GW = GROUP_WIDTH
C_AUV, C_AG = 0, 2 * _GW
C_RQ, C_RK, C_RV, C_RG = 3 * _GW, 4 * _GW, 5 * _GW, 6 * _GW
C_GQ = 7 * _GW
C_GK = C_GQ + _GW
C_GV = C_GK + KV_WIDTH
C_GG = C_GV + KV_WIDTH
C_SQ = C_GG + _GW
C_SK = C_SQ + _GW
C_SV = C_SK + KV_WIDTH
C_SG = C_SV + KV_WIDTH
IN_WIDTH = C_SG + _GW
QKV_WIDTH = 2 * _GW + 2 * KV_WIDTH

INPROJ_TILES = (640, 256)
OUTPROJ_TILES = (1280, 256)
ATTN_TQ = 256
ATTN_TK_ONLINE = 256
ATTN_TK_FAST = (3328, 256)
MAX_UNSHIFTED_SCORE = 32.0
WINDOW_CHUNKS_PER_STEP = (10, 2, 1)
VAT_ROWS = 80
VMEM_LIMIT = 56 * 1024 * 1024


def _silu(x):
    return x / (1.0 + jnp.exp(-x))


def _head_mask(width, head):
    lane = lax.broadcasted_iota(jnp.int32, (1, width), 1)
    return (lane // HEAD_DIM) == head


def _group_mean_sq(y, bd):
    sq = y * y
    hi = sq.astype(BF16)
    lo = (sq - hi.astype(F32)).astype(BF16)
    return (jnp.dot(hi, bd, preferred_element_type=F32)
            + jnp.dot(lo, bd, preferred_element_type=F32))


def _mod_kernel(cond_ref, w_ref, b_ref, o_ref):
    a = _silu(cond_ref[...])
    o_ref[0] = jnp.dot(a, w_ref[0], preferred_element_type=F32) + b_ref[0]


def _modulation(cond, w_mod, b_mod):
    depth, d, d3 = w_mod.shape
    nblk = d3 // d
    return pl.pallas_call(
        _mod_kernel,
        out_shape=jax.ShapeDtypeStruct((depth, 8, d3), F32),
        grid=(depth, nblk),
        in_specs=[
            pl.BlockSpec((8, d), lambda i, j: (0, 0)),
            pl.BlockSpec((1, d, d), lambda i, j: (i, 0, j)),
            pl.BlockSpec((1, 1, d), lambda i, j: (i, 0, j)),
        ],
        out_specs=pl.BlockSpec((1, 8, d), lambda i, j: (i, 0, j)),
        compiler_params=pltpu.CompilerParams(
            dimension_semantics=("arbitrary", "arbitrary"), vmem_limit_bytes=VMEM_LIMIT),
        name="modulation",
    )(cond, w_mod, b_mod.reshape(depth, 1, d3))


def _inproj_kernel(x_ref, mod_ref, gain_ref, w_ref, mix_ref, mbias_ref, bd_ref,
                   cos_ref, sina_ref, sinb_ref, qkg_ref,
                   mlp_ref, r_ref, g_ref, s_ref, *, ctx_len, d_model):
    b = pl.program_id(0)
    t = pl.program_id(1)
    d = d_model
    tm = x_ref.shape[1]
    gain = gain_ref[...]
    hs = []
    for c in range(tm // CHUNK):
        row = jnp.where(t * tm + c * CHUNK < ctx_len, 2, b)
        mod = mod_ref[pl.ds(row, 1), :]
        x = x_ref[0, c * CHUNK:(c + 1) * CHUNK, :]
        ms = jnp.mean(x * x, axis=-1, keepdims=True)
        xn = x * lax.rsqrt(ms + RMS_EPS) * gain
        hs.append((xn * (1.0 + mod[:, d:2 * d]) + mod[:, :d]).astype(BF16))
    h = jnp.concatenate(hs, axis=0)

    def proj(lo, width):
        return jnp.dot(h, w_ref[:, lo:lo + width], preferred_element_type=F32)

    bd = bd_ref[...]
    cos = cos_ref[...]
    sina = sina_ref[...]
    sinb = sinb_ref[...]

    def norm_rope_store(out_ref, q, q_ms, k, k_ms, gi):
        qg = qkg_ref[gi:gi + 1, :]
        kg = qkg_ref[gi + 1:gi + 2, :]
        parts = ((q[:, :128], q_ms[:, :128], qg), (q[:, 128:], q_ms[:, 128:], qg), (k, k_ms, kg))
        for j, (y, ms, gain_row) in enumerate(parts):
            yn = y * lax.rsqrt(ms + RMS_EPS) * gain_row
            out = yn * cos + pltpu.roll(yn, 128 - 16, 1) * sina + pltpu.roll(yn, 16, 1) * sinb
            out_ref[0, :, j * 128:(j + 1) * 128] = out.astype(out_ref.dtype)

    uv = jax.nn.gelu(proj(C_AUV, 2 * _GW), approximate=True)
    u = uv[:, :_GW]
    v = uv[:, _GW:].astype(BF16)
    ag = _silu(proj(C_AG, _GW))
    gq = proj(C_GQ, _GW)
    kk = jnp.concatenate([proj(C_GK, KV_WIDTH), proj(C_SK, KV_WIDTH)], axis=1)

    r_ref[0, :, 0:_GW] = proj(C_RQ, _GW).astype(r_ref.dtype)
    r_ref[0, :, _GW:2 * _GW] = (proj(C_RK, _GW) * ATTN_SCALE).astype(r_ref.dtype)
    r_ref[0, :, 2 * _GW:3 * _GW] = proj(C_RV, _GW).astype(r_ref.dtype)
    r_ref[0, :, 3 * _GW:4 * _GW] = _silu(proj(C_RG, _GW)).astype(r_ref.dtype)

    masks = [_head_mask(_GW, hh) for hh in range(N_HEADS)]
    for c in range(tm // CHUNK):
        rows = slice(c * CHUNK, (c + 1) * CHUNK)
        vc = v[rows]
        sv = mbias_ref[...]
        for hh in range(N_HEADS):
            mixed = jnp.dot(mix_ref[hh], vc, preferred_element_type=F32)
            sv = sv + jnp.where(masks[hh], mixed, 0.0)
        mlp_ref[0, rows, :] = (u[rows] * sv * ag[rows]).astype(mlp_ref.dtype)

    gq_ms = _group_mean_sq(gq, bd)
    kk_ms = _group_mean_sq(kk, bd)
    sq = proj(C_SQ, _GW)
    g_ref[0, :, 384:512] = proj(C_GV, KV_WIDTH).astype(g_ref.dtype)
    g_ref[0, :, 512:768] = _silu(proj(C_GG, _GW)).astype(g_ref.dtype)
    norm_rope_store(g_ref, gq, gq_ms, kk[:, :128], kk_ms[:, :128], 0)
    sq_ms = _group_mean_sq(sq, bd)
    s_ref[0, :, 384:512] = proj(C_SV, KV_WIDTH).astype(s_ref.dtype)
    s_ref[0, :, 512:768] = _silu(proj(C_SG, _GW)).astype(s_ref.dtype)
    norm_rope_store(s_ref, sq, sq_ms, kk[:, 128:], kk_ms[:, 128:], 2)


def _inproj(xs, mod, gain, w_in, mix, mbias, bd, cos, sina, sinb, qkg, *, ctx_len):
    bsz, t_len, d = xs.shape
    tm = next(n for n in INPROJ_TILES if t_len % n == 0)
    nt = t_len // tm
    const2 = lambda b, t: (0, 0)
    tok = lambda b, t: (b, t, 0)
    tab = lambda b, t: (t, 0)
    return pl.pallas_call(
        functools.partial(_inproj_kernel, ctx_len=ctx_len, d_model=d),
        out_shape=(
            jax.ShapeDtypeStruct((bsz, t_len, _GW), BF16),
            jax.ShapeDtypeStruct((bsz, t_len, 4 * _GW), BF16),
            jax.ShapeDtypeStruct((bsz, t_len, QKV_WIDTH), BF16),
            jax.ShapeDtypeStruct((bsz, t_len, QKV_WIDTH), BF16),
        ),
        grid=(bsz, nt),
        in_specs=[
            pl.BlockSpec((1, tm, d), tok),
            pl.BlockSpec((8, 3 * d), const2),
            pl.BlockSpec((1, d), const2),
            pl.BlockSpec((d, IN_WIDTH), const2),
            pl.BlockSpec((N_HEADS, CHUNK, CHUNK), lambda b, t: (0, 0, 0)),
            pl.BlockSpec((CHUNK, _GW), const2),
            pl.BlockSpec((_GW, _GW), const2),
            pl.BlockSpec((tm, 128), tab),
            pl.BlockSpec((tm, 128), tab),
            pl.BlockSpec((tm, 128), tab),
            pl.BlockSpec((4, 128), const2),
        ],
        out_specs=(
            pl.BlockSpec((1, tm, _GW), tok),
            pl.BlockSpec((1, tm, 4 * _GW), tok),
            pl.BlockSpec((1, tm, QKV_WIDTH), tok),
            pl.BlockSpec((1, tm, QKV_WIDTH), tok),
        ),
        compiler_params=pltpu.CompilerParams(
            dimension_semantics=("arbitrary", "arbitrary"), vmem_limit_bytes=VMEM_LIMIT),
        name="inproj",
    )(xs, mod, gain, w_in, mix, mbias, bd, cos, sina, sinb, qkg)


def _ret_chunk(q, k, v, dec_ref, qdec, kdec, cdec, state_ref):
    state = state_ref[...]
    y = jnp.dot(q, state.astype(BF16), preferred_element_type=F32) * qdec
    for hh in range(N_HEADS):
        mask = _head_mask(_GW, hh)
        qh = jnp.where(mask, q, jnp.zeros_like(q))
        sc = lax.dot_general(qh, k, (((1,), (1,)), ((), ())), preferred_element_type=F32)
        sc = (sc * dec_ref[hh]).astype(BF16)
        y = y + jnp.where(mask, jnp.dot(sc, v, preferred_element_type=F32), 0.0)
    kd = (k.astype(F32) * kdec).astype(BF16)
    kv = lax.dot_general(kd, v, (((0,), (0,)), ((), ())), preferred_element_type=F32)
    row_head = lax.broadcasted_iota(jnp.int32, (_GW, _GW), 0) // HEAD_DIM
    col_head = lax.broadcasted_iota(jnp.int32, (_GW, _GW), 1) // HEAD_DIM
    state_ref[...] = state * cdec + jnp.where(row_head == col_head, kv, 0.0)
    return y


def _ret_kernel(qf_ref, kf_ref, vf_ref, qb_ref, kb_ref, vb_ref,
                decf_ref, decb_ref, vec_ref, yf_ref, yb_ref, sf_ref, sb_ref):
    @pl.when(pl.program_id(1) == 0)
    def _():
        sf_ref[...] = jnp.zeros_like(sf_ref)
        sb_ref[...] = jnp.zeros_like(sb_ref)

    n = qf_ref.shape[1] // CHUNK
    for i in range(n):
        rf = slice(i * CHUNK, (i + 1) * CHUNK)
        yf_ref[0, rf, :] = _ret_chunk(qf_ref[0, rf, :], kf_ref[0, rf, :], vf_ref[0, rf, :], decf_ref,
                                      vec_ref[0], vec_ref[1], vec_ref[2, 0:1, :],
                                      sf_ref).astype(yf_ref.dtype)
        rb = slice((n - 1 - i) * CHUNK, (n - i) * CHUNK)
        yb_ref[0, rb, :] = _ret_chunk(qb_ref[0, rb, :], kb_ref[0, rb, :], vb_ref[0, rb, :], decb_ref,
                                      vec_ref[3], vec_ref[4], vec_ref[5, 0:1, :],
                                      sb_ref).astype(yb_ref.dtype)


def _retention(r, decf, decb, vecs, *, ctx_len):
    bsz, t_len, _ = r.shape
    rows = ctx_len
    nb = t_len // rows

    def bwd_block(t):
        return jnp.where(t < 1, 0, nb - t)

    def fwd(col):
        return pl.BlockSpec((1, rows, _GW), lambda b, t: (b, t, col))

    def bwd(col):
        return pl.BlockSpec((1, rows, _GW), lambda b, t: (b, bwd_block(t), col))

    return pl.pallas_call(
        _ret_kernel,
        out_shape=(jax.ShapeDtypeStruct((bsz, t_len, _GW), BF16),
                   jax.ShapeDtypeStruct((bsz, t_len, _GW), BF16)),
        grid=(bsz, nb),
        in_specs=[fwd(0), fwd(1), fwd(2), bwd(0), bwd(1), bwd(2),
                  pl.BlockSpec((N_HEADS, CHUNK, CHUNK), lambda b, t: (0, 0, 0)),
                  pl.BlockSpec((N_HEADS, CHUNK, CHUNK), lambda b, t: (0, 0, 0)),
                  pl.BlockSpec((6, CHUNK, _GW), lambda b, t: (0, 0, 0))],
        out_specs=(pl.BlockSpec((1, rows, _GW), lambda b, t: (b, t, 0)),
                   pl.BlockSpec((1, rows, _GW), lambda b, t: (b, bwd_block(t), 0))),
        scratch_shapes=[pltpu.VMEM((_GW, _GW), F32), pltpu.VMEM((_GW, _GW), F32)],
        compiler_params=pltpu.CompilerParams(
            dimension_semantics=("arbitrary", "arbitrary"), vmem_limit_bytes=VMEM_LIMIT),
        name="retention",
    )(r, r, r, r, r, r, decf, decb, vecs)


def _retention_tables(lg_f, lg_b):
    pos = jnp.arange(CHUNK, dtype=F32)
    diff = pos[:, None] - pos[None, :]
    keep_f = diff >= 0
    dec_f = jnp.where(keep_f, jnp.exp(lg_f[:, None, None] * jnp.where(keep_f, diff, 0.0)), 0.0)
    keep_b = diff < 0
    dec_b = jnp.where(keep_b, jnp.exp(lg_b[:, None, None] * jnp.where(keep_b, -diff, 0.0)), 0.0)

    def lanes(tab):
        return jnp.repeat(tab.T, HEAD_DIM, axis=1)

    qdec_f = lanes(jnp.exp(lg_f[:, None] * (pos + 1.0)))
    kdec_f = lanes(jnp.exp(lg_f[:, None] * (CHUNK - 1.0 - pos)))
    qdec_b = lanes(jnp.exp(lg_b[:, None] * (CHUNK - pos)))
    kdec_b = lanes(jnp.exp(lg_b[:, None] * pos))
    cdec_f = jnp.broadcast_to(jnp.repeat(jnp.exp(lg_f * CHUNK), HEAD_DIM)[None, :], (CHUNK, _GW))
    cdec_b = jnp.broadcast_to(jnp.repeat(jnp.exp(lg_b * CHUNK), HEAD_DIM)[None, :], (CHUNK, _GW))
    vecs = jnp.stack([qdec_f, kdec_f, cdec_f, qdec_b, kdec_b, cdec_b]).astype(F32)
    return dec_f.astype(F32), dec_b.astype(F32), vecs


def _gattn_online_kernel(q_ref, kt_ref, va_ref, gate_ref, o_ref, m_ref, acc_ref, *,
                         tk, n_kt, ctx_qtiles, ctx_kt):
    qi = pl.program_id(2)
    tq = q_ref.shape[1]
    q = q_ref[0]
    q2 = jnp.concatenate([q[:, :HEAD_DIM], q[:, HEAD_DIM:]], axis=0)
    m_ref[...] = jnp.full(m_ref.shape, NEG_INF, F32)
    acc_ref[...] = jnp.zeros_like(acc_ref)

    def body(kt, carry):
        start = pl.multiple_of(kt * tk, tk)
        k = kt_ref[0, 0, :, pl.ds(start, tk)]
        v = va_ref[0, 0, pl.ds(start, tk), :]
        s = jnp.dot(q2, k, preferred_element_type=F32)
        m_prev = m_ref[...]
        m_new = jnp.maximum(m_prev, jnp.max(s, axis=-1, keepdims=True))
        p = jnp.exp(s - m_new).astype(BF16)
        acc_ref[...] = (jnp.exp(m_prev - m_new) * acc_ref[...]
                        + jnp.dot(p, v, preferred_element_type=F32))
        m_ref[...] = m_new
        return carry

    lax.fori_loop(0, jnp.where(qi < ctx_qtiles, ctx_kt, n_kt), body, 0)
    acc = acc_ref[...]
    o = acc[:, :HEAD_DIM] / acc[:, HEAD_DIM:HEAD_DIM + 1]
    out = jnp.concatenate([o[:tq], o[tq:]], axis=1)
    o_ref[0] = (out * gate_ref[0].astype(F32)).astype(o_ref.dtype)


def _global_attention_online(g, *, ctx_len):
    bsz, t_len, _ = g.shape
    tq, tk = ATTN_TQ, ATTN_TK_ONLINE
    k = g[:, :, _GW:_GW + KV_WIDTH].reshape(bsz, t_len, N_KV, HEAD_DIM)
    v = g[:, :, _GW + KV_WIDTH:_GW + 2 * KV_WIDTH].reshape(bsz, t_len, N_KV, HEAD_DIM)
    kt = jnp.transpose(k, (0, 2, 3, 1))
    ones = jnp.ones((bsz, N_KV, t_len, 1), v.dtype)
    zeros = jnp.zeros((bsz, N_KV, t_len, 128 - HEAD_DIM - 1), v.dtype)
    va = jnp.concatenate([jnp.transpose(v, (0, 2, 1, 3)), ones, zeros], axis=-1)
    return pl.pallas_call(
        functools.partial(_gattn_online_kernel, tk=tk, n_kt=t_len // tk,
                          ctx_qtiles=ctx_len // tq, ctx_kt=ctx_len // tk),
        out_shape=jax.ShapeDtypeStruct((bsz, t_len, _GW), BF16),
        grid=(bsz, N_KV, t_len // tq),
        in_specs=[
            pl.BlockSpec((1, tq, 128), lambda b, gi, qi: (b, qi, gi)),
            pl.BlockSpec((1, 1, HEAD_DIM, t_len), lambda b, gi, qi: (b, gi, 0, 0)),
            pl.BlockSpec((1, 1, t_len, 128), lambda b, gi, qi: (b, gi, 0, 0)),
            pl.BlockSpec((1, tq, 128), lambda b, gi, qi: (b, qi, 4 + gi)),
        ],
        out_specs=pl.BlockSpec((1, tq, 128), lambda b, gi, qi: (b, qi, gi)),
        scratch_shapes=[pltpu.VMEM((2 * tq, 1), F32), pltpu.VMEM((2 * tq, 128), F32)],
        compiler_params=pltpu.CompilerParams(
            dimension_semantics=("arbitrary", "arbitrary", "arbitrary"),
            vmem_limit_bytes=VMEM_LIMIT),
        name="global_attention_online",
    )(g, kt, va, g)


def _gattn_kernel(q_ref, k_ref, vat_ref, gate_ref, o_ref, acc_ref, *, tk, n_kt, ctx_qtiles, ctx_len):
    gi = pl.program_id(1)
    qi = pl.program_id(2)
    tq = q_ref.shape[1]
    qt = q_ref[0].T
    top = jnp.concatenate([qt[:HEAD_DIM], qt[HEAD_DIM:]], axis=1)
    zero = jnp.zeros_like(top)
    w = jnp.where(gi == 0, jnp.concatenate([top, zero], axis=0), jnp.concatenate([zero, top], axis=0))

    def tile(start, width):
        k = k_ref[0, pl.ds(start, width), :]
        vat = vat_ref[0, 0, :, pl.ds(start, width)]
        st = jnp.dot(k, w, preferred_element_type=F32)
        acc_ref[...] += jnp.dot(vat, jnp.exp(st).astype(BF16), preferred_element_type=F32)

    acc_ref[...] = jnp.zeros_like(acc_ref)

    @pl.when(qi < ctx_qtiles)
    def _():
        tile(0, ctx_len)

    @pl.when(qi >= ctx_qtiles)
    def _():
        for kt in range(n_kt):
            tile(kt * tk, tk)

    acc = acc_ref[...]
    ot = acc[:HEAD_DIM] / acc[HEAD_DIM:HEAD_DIM + 1]
    out = jnp.concatenate([ot[:, :tq].T, ot[:, tq:].T], axis=1)
    o_ref[0] = (out * gate_ref[0].astype(F32)).astype(o_ref.dtype)


def _global_attention_fast(g, *, ctx_len):
    bsz, t_len, _ = g.shape
    tq = ATTN_TQ
    tk = next(w for w in ATTN_TK_FAST if t_len % w == 0)
    return pl.pallas_call(
        functools.partial(_gattn_kernel, tk=tk, n_kt=t_len // tk,
                          ctx_qtiles=ctx_len // tq, ctx_len=ctx_len),
        out_shape=jax.ShapeDtypeStruct((bsz, t_len, _GW), BF16),
        grid=(bsz, N_KV, t_len // tq),
        in_specs=[
            pl.BlockSpec((1, tq, 128), lambda b, gi, qi: (b, qi, gi)),
            pl.BlockSpec((1, t_len, KV_WIDTH), lambda b, gi, qi: (b, 0, _GW // KV_WIDTH)),
            pl.BlockSpec((1, 1, VAT_ROWS, t_len), lambda b, gi, qi: (b, gi, 0, 0)),
            pl.BlockSpec((1, tq, 128), lambda b, gi, qi: (b, qi, 4 + gi)),
        ],
        out_specs=pl.BlockSpec((1, tq, 128), lambda b, gi, qi: (b, qi, gi)),
        scratch_shapes=[pltpu.VMEM((VAT_ROWS, 2 * tq), F32)],
        compiler_params=pltpu.CompilerParams(
            dimension_semantics=("arbitrary", "arbitrary", "arbitrary"),
            vmem_limit_bytes=VMEM_LIMIT),
        name="global_attention",
    )(g, g, _vt_ones(g), g)


def _global_attention(g, score_bound, *, ctx_len):
    return lax.cond(score_bound <= MAX_UNSHIFTED_SCORE,
                    functools.partial(_global_attention_fast, ctx_len=ctx_len),
                    functools.partial(_global_attention_online, ctx_len=ctx_len), g)


def _wattn_kernel(sink_ref, q_ref, k_ref, vat_ref, gate_ref, o_ref, *,
                  chunks_per_step, ctx_len, t_len):
    n_loc = 3 * CHUNK
    lane_head = lax.broadcasted_iota(jnp.int32, (1, N_HEADS * CHUNK), 1) // CHUNK
    sink = jnp.zeros((1, N_HEADS * CHUNK), F32)
    for hh in range(N_HEADS):
        sink = jnp.where(lane_head == hh, sink_ref[hh], sink)
    k_ctx = k_ref[0, 0:ctx_len, :]
    zero = jnp.zeros((HEAD_DIM, CHUNK), BF16)

    def chunk(i, carry):
        c = pl.program_id(1) * chunks_per_step + i
        rows = pl.ds(pl.multiple_of(i * CHUNK, CHUNK), CHUNK)
        qt = q_ref[0, rows, :].T
        blk = [qt[hh * HEAD_DIM:(hh + 1) * HEAD_DIM] for hh in range(N_HEADS)]
        w = jnp.concatenate([jnp.concatenate([blk[0], blk[1], zero, zero], axis=1),
                             jnp.concatenate([zero, zero, blk[2], blk[3]], axis=1)], axis=0)
        start = pl.multiple_of(jnp.clip((c - 1) * CHUNK, 0, t_len - n_loc), CHUNK)
        st_loc = jnp.dot(k_ref[0, pl.ds(start, n_loc), :], w, preferred_element_type=F32)
        st_ctx = jnp.dot(k_ctx, w, preferred_element_type=F32)
        kpos = start + lax.broadcasted_iota(jnp.int32, (n_loc, CHUNK), 0)
        qpos = c * CHUNK + lax.broadcasted_iota(jnp.int32, (n_loc, CHUNK), 1)
        valid = (jnp.abs(qpos - kpos) <= WINDOW) & (kpos >= ctx_len) & (qpos >= ctx_len)
        valid = jnp.concatenate([valid] * N_HEADS, axis=1)
        st_loc = jnp.where(valid, st_loc, NEG_INF)
        m = jnp.maximum(jnp.maximum(jnp.max(st_loc, axis=0, keepdims=True),
                                    jnp.max(st_ctx, axis=0, keepdims=True)), sink)
        p_loc = jnp.exp(st_loc - m).astype(BF16)
        p_ctx = jnp.exp(st_ctx - m).astype(BF16)
        e_sink = jnp.exp(sink - m)
        outs = []
        for gi in range(N_KV):
            cols = slice(gi * 2 * CHUNK, (gi + 1) * 2 * CHUNK)
            acc = (jnp.dot(vat_ref[0, gi, :, pl.ds(start, n_loc)], p_loc[:, cols],
                           preferred_element_type=F32)
                   + jnp.dot(vat_ref[0, gi, :, 0:ctx_len], p_ctx[:, cols],
                             preferred_element_type=F32))
            ot = acc[:HEAD_DIM] / (acc[HEAD_DIM:HEAD_DIM + 1] + e_sink[:, cols])
            outs += [ot[:, :CHUNK].T, ot[:, CHUNK:].T]
        out = jnp.concatenate(outs, axis=1)
        o_ref[0, rows, :] = (out * gate_ref[0, rows, :].astype(F32)).astype(o_ref.dtype)
        return carry

    lax.fori_loop(0, chunks_per_step, chunk, 0, unroll=5 if chunks_per_step % 5 == 0 else 2)


def _vt_ones(qkv):
    bsz, t_len, _ = qkv.shape
    v = qkv[:, :, _GW + KV_WIDTH:_GW + 2 * KV_WIDTH].reshape(bsz, t_len, N_KV, HEAD_DIM)
    return jnp.concatenate([jnp.transpose(v, (0, 2, 3, 1)),
                            jnp.ones((bsz, N_KV, 1, t_len), v.dtype),
                            jnp.zeros((bsz, N_KV, VAT_ROWS - HEAD_DIM - 1, t_len), v.dtype)], axis=2)


def _window_attention(s, sink, *, ctx_len):
    bsz, t_len, _ = s.shape
    nc = t_len // CHUNK
    cps = next(n for n in WINDOW_CHUNKS_PER_STEP if nc % n == 0)
    rows = cps * CHUNK
    return pl.pallas_call(
        functools.partial(_wattn_kernel, chunks_per_step=cps, ctx_len=ctx_len, t_len=t_len),
        out_shape=jax.ShapeDtypeStruct((bsz, t_len, _GW), BF16),
        grid=(bsz, nc // cps),
        in_specs=[
            pl.BlockSpec(memory_space=pltpu.SMEM),
            pl.BlockSpec((1, rows, _GW), lambda b, j: (b, j, 0)),
            pl.BlockSpec((1, t_len, KV_WIDTH), lambda b, j: (b, 0, _GW // KV_WIDTH)),
            pl.BlockSpec((1, N_KV, VAT_ROWS, t_len), lambda b, j: (b, 0, 0, 0)),
            pl.BlockSpec((1, rows, _GW), lambda b, j: (b, j, 2)),
        ],
        out_specs=pl.BlockSpec((1, rows, _GW), lambda b, j: (b, j, 0)),
        compiler_params=pltpu.CompilerParams(
            dimension_semantics=("arbitrary", "arbitrary"), vmem_limit_bytes=VMEM_LIMIT),
        name="window_attention",
    )(sink, s, s, _vt_ones(s), s)


def _outproj_kernel(x_ref, mod_ref, mlp_ref, yf_ref, yb_ref, rg_ref, ga_ref, sw_ref,
                    rn_ref, bd_ref, w_ref, o_ref, *, ctx_len, d_model):
    b = pl.program_id(0)
    t = pl.program_id(1)
    tm = x_ref.shape[1]
    y = yf_ref[0].astype(F32) + yb_ref[0].astype(F32)
    bd = bd_ref[...]
    ms = _group_mean_sq(y, bd)
    ret = (y * lax.rsqrt(ms + RMS_EPS) * rn_ref[...]) * rg_ref[0].astype(F32)
    acc = jnp.dot(mlp_ref[0], w_ref[0:_GW, :], preferred_element_type=F32)
    acc += jnp.dot(ret.astype(BF16), w_ref[_GW:2 * _GW, :], preferred_element_type=F32)
    acc += jnp.dot(ga_ref[0], w_ref[2 * _GW:3 * _GW, :], preferred_element_type=F32)
    acc += jnp.dot(sw_ref[0], w_ref[3 * _GW:4 * _GW, :], preferred_element_type=F32)
    for c in range(tm // CHUNK):
        row = jnp.where(t * tm + c * CHUNK < ctx_len, 2, b)
        gate = mod_ref[pl.ds(row, 1), :][:, 2 * d_model:]
        rows = slice(c * CHUNK, (c + 1) * CHUNK)
        o_ref[0, rows, :] = x_ref[0, rows, :] + gate * acc[rows]


def _outproj(xs, mod, mlp, yf, yb, r, ga, sw, rn, bd, w_out, *, ctx_len):
    bsz, t_len, d = xs.shape
    tm = next(n for n in OUTPROJ_TILES if t_len % n == 0)
    const2 = lambda b, t: (0, 0)
    tok = lambda b, t: (b, t, 0)
    grp = pl.BlockSpec((1, tm, _GW), tok)
    return pl.pallas_call(
        functools.partial(_outproj_kernel, ctx_len=ctx_len, d_model=d),
        out_shape=jax.ShapeDtypeStruct(xs.shape, xs.dtype),
        grid=(bsz, t_len // tm),
        in_specs=[
            pl.BlockSpec((1, tm, d), tok),
            pl.BlockSpec((8, 3 * d), const2),
            grp, grp, grp,
            pl.BlockSpec((1, tm, _GW), lambda b, t: (b, t, 3)),
            grp, grp,
            pl.BlockSpec((1, _GW), const2),
            pl.BlockSpec((_GW, _GW), const2),
            pl.BlockSpec((4 * _GW, d), const2),
        ],
        out_specs=pl.BlockSpec((1, tm, d), tok),
        input_output_aliases={0: 0},
        compiler_params=pltpu.CompilerParams(
            dimension_semantics=("arbitrary", "arbitrary"), vmem_limit_bytes=VMEM_LIMIT),
        name="outproj",
    )(xs, mod, mlp, yf, yb, r, ga, sw, rn, bd, w_out)


def _rope_tables(seq_len, ctx_len):
    rows = seq_len // GRID_W
    row = jnp.broadcast_to(jnp.arange(rows, dtype=F32)[:, None], (rows, GRID_W)).reshape(-1)
    col = jnp.broadcast_to(jnp.arange(GRID_W, dtype=F32)[None, :], (rows, GRID_W)).reshape(-1)
    half = HEAD_DIM // 2
    inv_freq = 1.0 / (ROPE_BASE ** (jnp.arange(0, half, 2, dtype=F32) / half))
    ang_r = row[:, None] * inv_freq[None, :]
    ang_c = col[:, None] * inv_freq[None, :]
    ang = jnp.concatenate([ang_r, ang_r, ang_c, ang_c], axis=-1)
    cos = jnp.concatenate([jnp.ones((ctx_len, HEAD_DIM), F32), jnp.cos(ang)], axis=0)
    sin = jnp.concatenate([jnp.zeros((ctx_len, HEAD_DIM), F32), jnp.sin(ang)], axis=0)
    first = (jnp.arange(HEAD_DIM) % (half)) < (half // 2)
    sina = jnp.where(first[None, :], -sin, 0.0)
    sinb = jnp.where(first[None, :], 0.0, sin)
    two = lambda a: jnp.concatenate([a, a], axis=-1)
    return two(cos), two(sina), two(sinb)


def kernel(x, c, ctx, c_ctx, norm_gain, w_mod, b_mod, w_in, w_out, mlp_mix, mlp_bias,
           ret_decay_fwd, ret_decay_bwd, ret_norm, attn_q_norm, attn_k_norm,
           swa_q_norm, swa_k_norm, swa_sink):
    bsz, seq_len, d = x.shape
    ctx_len = ctx.shape[1]
    depth = w_in.shape[0]
    assert w_in.shape[2] == IN_WIDTH and w_out.shape[1] == 4 * _GW
    assert ctx_len % ATTN_TQ == 0 and seq_len % ctx_len == 0 and bsz <= 2

    cond = jnp.zeros((8, d), F32).at[:bsz].set(c).at[2].set(c_ctx)
    mod_all = _modulation(cond, w_mod, b_mod)

    cos, sina, sinb = _rope_tables(seq_len, ctx_len)
    lane_group = jnp.arange(_GW) // HEAD_DIM
    bd = ((lane_group[:, None] == lane_group[None, :]).astype(F32) * (1.0 / HEAD_DIM)).astype(BF16)
    w_in_b = w_in.astype(BF16)
    w_out_b = w_out.astype(BF16)
    mix_b = mlp_mix.astype(BF16)
    two = lambda a: jnp.concatenate([a, a], axis=-1)

    xs = jnp.concatenate([ctx, x], axis=1)
    for i in range(depth):
        mod = mod_all[i]
        mbias = jnp.repeat(mlp_bias[i].T, HEAD_DIM, axis=1)
        qkg = jnp.stack([two(attn_q_norm[i]) * ATTN_SCALE, two(attn_k_norm[i]),
                         two(swa_q_norm[i]) * ATTN_SCALE, two(swa_k_norm[i])]).astype(F32)
        mlp, r, g, s = _inproj(xs, mod, norm_gain[i][None, :], w_in_b[i], mix_b[i], mbias, bd,
                               cos, sina, sinb, qkg, ctx_len=ctx_len)
        lg_f = -jnp.exp(ret_decay_fwd[i].astype(F32))
        lg_b = -jnp.exp(ret_decay_bwd[i].astype(F32))
        decf, decb, vecs = _retention_tables(lg_f, lg_b)
        yf, yb = _retention(r, decf, decb, vecs, ctx_len=ctx_len)
        score_bound = (HEAD_DIM * ATTN_SCALE * jnp.max(jnp.abs(attn_q_norm[i]))
                       * jnp.max(jnp.abs(attn_k_norm[i])))
        ga = _global_attention(g, score_bound, ctx_len=ctx_len)
        sw = _window_attention(s, swa_sink[i].astype(F32), ctx_len=ctx_len)
        xs = _outproj(xs, mod, mlp, yf, yb, r, ga, sw, ret_norm[i].reshape(1, _GW), bd,
                      w_out_b[i], ctx_len=ctx_len)
    return xs[:, ctx_len:, :]
```

```python
import functools

import numpy as np
import jax
import jax.numpy as jnp
from jax import lax
from jax.experimental import pallas as pl
from jax.experimental.pallas import tpu as pltpu

F32 = jnp.float32
BF16 = jnp.bfloat16

GRID_W = 64
CHUNK = 128
WINDOW = 128
HEAD_DIM = 64
N_HEADS = 4
GROUP_WIDTH = N_HEADS * HEAD_DIM
N_KV = 2
KV_WIDTH = N_KV * HEAD_DIM
ROPE_BASE = 10000.0
RMS_EPS = 1e-6
ATTN_SCALE = HEAD_DIM ** -0.5
NEG_INF = -1e30

_GW = GROUP_WIDTH
C_AUV, C_AG = 0, 2 * _GW
C_RQ, C_RK, C_RV, C_RG = 3 * _GW, 4 * _GW, 5 * _GW, 6 * _GW
C_GQ = 7 * _GW
C_GK = C_GQ + _GW
C_GV = C_GK + KV_WIDTH
C_GG = C_GV + KV_WIDTH
C_SQ = C_GG + _GW
C_SK = C_SQ + _GW
C_SV = C_SK + KV_WIDTH
C_SG = C_SV + KV_WIDTH
IN_WIDTH = C_SG + _GW
QKV_WIDTH = 2 * _GW + 2 * KV_WIDTH

INPROJ_TILES = (640, 256)
OUTPROJ_TILES = (1280, 256)
ATTN_TQ = 256
ATTN_TK_ONLINE = 256
ATTN_TK_FAST = (3328, 256)
MAX_UNSHIFTED_SCORE = 32.0
WINDOW_CHUNKS_PER_STEP = (10, 2, 1)
VAT_ROWS = 80
VMEM_LIMIT = 56 * 1024 * 1024


def _silu(x):
    return x / (1.0 + jnp.exp(-x))


def _head_mask(width, head):
    lane = lax.broadcasted_iota(jnp.int32, (1, width), 1)
    return (lane // HEAD_DIM) == head


def _group_mean_sq(y, bd):
    sq = y * y
    hi = sq.astype(BF16)
    lo = (sq - hi.astype(F32)).astype(BF16)
    return (jnp.dot(hi, bd, preferred_element_type=F32)
            + jnp.dot(lo, bd, preferred_element_type=F32))


def _mod_kernel(cond_ref, w_ref, b_ref, o_ref):
    a = _silu(cond_ref[...])
    o_ref[0] = jnp.dot(a, w_ref[0], preferred_element_type=F32) + b_ref[0]


def _modulation(cond, w_mod, b_mod):
    depth, d, d3 = w_mod.shape
    nblk = d3 // d
    return pl.pallas_call(
        _mod_kernel,
        out_shape=jax.ShapeDtypeStruct((depth, 8, d3), F32),
        grid=(depth, nblk),
        in_specs=[
            pl.BlockSpec((8, d), lambda i, j: (0, 0)),
            pl.BlockSpec((1, d, d), lambda i, j: (i, 0, j)),
            pl.BlockSpec((1, 1, d), lambda i, j: (i, 0, j)),
        ],
        out_specs=pl.BlockSpec((1, 8, d), lambda i, j: (i, 0, j)),
        compiler_params=pltpu.CompilerParams(
            dimension_semantics=("arbitrary", "arbitrary"), vmem_limit_bytes=VMEM_LIMIT),
        name="modulation",
    )(cond, w_mod, b_mod.reshape(depth, 1, d3))


def _inproj_kernel(x_ref, mod_ref, gain_ref, w_ref, mix_ref, mbias_ref, bd_ref,
                   cos_ref, sina_ref, sinb_ref, qkg_ref,
                   mlp_ref, r_ref, g_ref, s_ref, gvt_ref, svt_ref, *, ctx_len, d_model):
    b = pl.program_id(0)
    t = pl.program_id(1)
    d = d_model
    tm = x_ref.shape[1]
    gain = gain_ref[...]
    hs = []
    for c in range(tm // CHUNK):
        row = jnp.where(t * tm + c * CHUNK < ctx_len, 2, b)
        mod = mod_ref[pl.ds(row, 1), :]
        x = x_ref[0, c * CHUNK:(c + 1) * CHUNK, :]
        ms = jnp.mean(x * x, axis=-1, keepdims=True)
        xn = x * lax.rsqrt(ms + RMS_EPS) * gain
        hs.append((xn * (1.0 + mod[:, d:2 * d]) + mod[:, :d]).astype(BF16))
    h = jnp.concatenate(hs, axis=0)

    def proj(lo, width):
        return jnp.dot(h, w_ref[:, lo:lo + width], preferred_element_type=F32)

    bd = bd_ref[...]
    cos = cos_ref[...]
    sina = sina_ref[...]
    sinb = sinb_ref[...]

    def norm_rope_store(out_ref, q, q_ms, k, k_ms, gi):
        qg = qkg_ref[gi:gi + 1, :]
        kg = qkg_ref[gi + 1:gi + 2, :]
        parts = ((q[:, :128], q_ms[:, :128], qg), (q[:, 128:], q_ms[:, 128:], qg), (k, k_ms, kg))
        for j, (y, ms, gain_row) in enumerate(parts):
            yn = y * lax.rsqrt(ms + RMS_EPS) * gain_row
            out = yn * cos + pltpu.roll(yn, 128 - 16, 1) * sina + pltpu.roll(yn, 16, 1) * sinb
            out_ref[0, :, j * 128:(j + 1) * 128] = out.astype(out_ref.dtype)

    pad_row = lax.broadcasted_iota(jnp.int32, (VAT_ROWS - HEAD_DIM, tm), 0)
    ones_then_zeros = jnp.where(pad_row == 0, 1.0, 0.0).astype(BF16)

    def store_v(out_ref, vt_ref, v):
        out_ref[0, :, 384:512] = v.astype(out_ref.dtype)
        vt = v.T.astype(vt_ref.dtype)
        for gi in range(N_KV):
            vt_ref[0, gi, 0:HEAD_DIM, :] = vt[gi * HEAD_DIM:(gi + 1) * HEAD_DIM]
            vt_ref[0, gi, HEAD_DIM:VAT_ROWS, :] = ones_then_zeros

    uv = jax.nn.gelu(proj(C_AUV, 2 * _GW), approximate=True)
    u = uv[:, :_GW]
    v = uv[:, _GW:].astype(BF16)
    ag = _silu(proj(C_AG, _GW))
    gq = proj(C_GQ, _GW)
    kk = jnp.concatenate([proj(C_GK, KV_WIDTH), proj(C_SK, KV_WIDTH)], axis=1)

    r_ref[0, :, 0:_GW] = proj(C_RQ, _GW).astype(r_ref.dtype)
    r_ref[0, :, _GW:2 * _GW] = (proj(C_RK, _GW) * ATTN_SCALE).astype(r_ref.dtype)
    r_ref[0, :, 2 * _GW:3 * _GW] = proj(C_RV, _GW).astype(r_ref.dtype)
    r_ref[0, :, 3 * _GW:4 * _GW] = _silu(proj(C_RG, _GW)).astype(r_ref.dtype)

    masks = [_head_mask(_GW, hh) for hh in range(N_HEADS)]
    for c in range(tm // CHUNK):
        rows = slice(c * CHUNK, (c + 1) * CHUNK)
        vc = v[rows]
        sv = mbias_ref[...]
        for hh in range(N_HEADS):
            mixed = jnp.dot(mix_ref[hh], vc, preferred_element_type=F32)
            sv = sv + jnp.where(masks[hh], mixed, 0.0)
        mlp_ref[0, rows, :] = (u[rows] * sv * ag[rows]).astype(mlp_ref.dtype)

    gq_ms = _group_mean_sq(gq, bd)
    kk_ms = _group_mean_sq(kk, bd)
    sq = proj(C_SQ, _GW)
    store_v(g_ref, gvt_ref, proj(C_GV, KV_WIDTH))
    g_ref[0, :, 512:768] = _silu(proj(C_GG, _GW)).astype(g_ref.dtype)
    norm_rope_store(g_ref, gq, gq_ms, kk[:, :128], kk_ms[:, :128], 0)
    sq_ms = _group_mean_sq(sq, bd)
    store_v(s_ref, svt_ref, proj(C_SV, KV_WIDTH))
    s_ref[0, :, 512:768] = _silu(proj(C_SG, _GW)).astype(s_ref.dtype)
    norm_rope_store(s_ref, sq, sq_ms, kk[:, 128:], kk_ms[:, 128:], 2)


def _inproj(xs, mod, gain, w_in, mix, mbias, bd, cos, sina, sinb, qkg, *, ctx_len):
    bsz, t_len, d = xs.shape
    tm = next(n for n in INPROJ_TILES if t_len % n == 0)
    nt = t_len // tm
    const2 = lambda b, t: (0, 0)
    tok = lambda b, t: (b, t, 0)
    tab = lambda b, t: (t, 0)
    return pl.pallas_call(
        functools.partial(_inproj_kernel, ctx_len=ctx_len, d_model=d),
        out_shape=(
            jax.ShapeDtypeStruct((bsz, t_len, _GW), BF16),
            jax.ShapeDtypeStruct((bsz, t_len, 4 * _GW), BF16),
            jax.ShapeDtypeStruct((bsz, t_len, QKV_WIDTH), BF16),
            jax.ShapeDtypeStruct((bsz, t_len, QKV_WIDTH), BF16),
            jax.ShapeDtypeStruct((bsz, N_KV, VAT_ROWS, t_len), BF16),
            jax.ShapeDtypeStruct((bsz, N_KV, VAT_ROWS, t_len), BF16),
        ),
        grid=(bsz, nt),
        in_specs=[
            pl.BlockSpec((1, tm, d), tok),
            pl.BlockSpec((8, 3 * d), const2),
            pl.BlockSpec((1, d), const2),
            pl.BlockSpec((d, IN_WIDTH), const2),
            pl.BlockSpec((N_HEADS, CHUNK, CHUNK), lambda b, t: (0, 0, 0)),
            pl.BlockSpec((CHUNK, _GW), const2),
            pl.BlockSpec((_GW, _GW), const2),
            pl.BlockSpec((tm, 128), tab),
            pl.BlockSpec((tm, 128), tab),
            pl.BlockSpec((tm, 128), tab),
            pl.BlockSpec((4, 128), const2),
        ],
        out_specs=(
            pl.BlockSpec((1, tm, _GW), tok),
            pl.BlockSpec((1, tm, 4 * _GW), tok),
            pl.BlockSpec((1, tm, QKV_WIDTH), tok),
            pl.BlockSpec((1, tm, QKV_WIDTH), tok),
            pl.BlockSpec((1, N_KV, VAT_ROWS, tm), lambda b, t: (b, 0, 0, t)),
            pl.BlockSpec((1, N_KV, VAT_ROWS, tm), lambda b, t: (b, 0, 0, t)),
        ),
        compiler_params=pltpu.CompilerParams(
            dimension_semantics=("arbitrary", "arbitrary"), vmem_limit_bytes=VMEM_LIMIT),
        name="inproj",
    )(xs, mod, gain, w_in, mix, mbias, bd, cos, sina, sinb, qkg)


def _ret_kernel(qf_ref, kf_ref, vf_ref, qb_ref, kb_ref, vb_ref,
                decf_ref, decb_ref, vec_ref, yf_ref, yb_ref, sf_ref, sb_ref):
    @pl.when(pl.program_id(1) == 0)
    def _():
        sf_ref[...] = jnp.zeros_like(sf_ref)
        sb_ref[...] = jnp.zeros_like(sb_ref)

    n = qf_ref.shape[1] // CHUNK
    masks = [_head_mask(_GW, hh) for hh in range(N_HEADS)]
    nt = (((1,), (1,)), ((), ()))
    tn = (((0,), (0,)), ((), ()))
    fwd = (qf_ref, kf_ref, vf_ref, decf_ref, 0, yf_ref, sf_ref)
    bwd = (qb_ref, kb_ref, vb_ref, decb_ref, 3, yb_ref, sb_ref)
    work = []
    for i in range(n):
        work.append((fwd, slice(i * CHUNK, (i + 1) * CHUNK)))
        work.append((bwd, slice((n - 1 - i) * CHUNK, (n - i) * CHUNK)))

    scores, kvs = [], []
    for (q_ref, k_ref, v_ref, _, vi, _, _), rows in work:
        q, k, v = q_ref[0, rows, :], k_ref[0, rows, :], v_ref[0, rows, :]
        scores.append([lax.dot_general(jnp.where(m, q, jnp.zeros_like(q)), k, nt,
                                       preferred_element_type=F32) for m in masks])
        kd = (k.astype(F32) * vec_ref[vi + 1]).astype(BF16)
        kvs.append(lax.dot_general(kd, v, tn, preferred_element_type=F32))
    intras = []
    for ((_, _, v_ref, dec_ref, _, _, _), rows), sc in zip(work, scores):
        v = v_ref[0, rows, :]
        y = jnp.zeros((CHUNK, _GW), F32)
        for hh in range(N_HEADS):
            p = (sc[hh] * dec_ref[hh]).astype(BF16)
            y = y + jnp.where(masks[hh], jnp.dot(p, v, preferred_element_type=F32), 0.0)
        intras.append(y)
    row_head = lax.broadcasted_iota(jnp.int32, (_GW, _GW), 0) // HEAD_DIM
    col_head = lax.broadcasted_iota(jnp.int32, (_GW, _GW), 1) // HEAD_DIM
    for ((q_ref, _, _, _, vi, y_ref, state_ref), rows), intra, kv in zip(work, intras, kvs):
        state = state_ref[...]
        inter = jnp.dot(q_ref[0, rows, :], state.astype(BF16), preferred_element_type=F32)
        y_ref[0, rows, :] = (intra + inter * vec_ref[vi]).astype(y_ref.dtype)
        state_ref[...] = state * vec_ref[vi + 2, 0:1, :] + jnp.where(row_head == col_head, kv, 0.0)


def _retention(r, decf, decb, vecs, *, ctx_len):
    bsz, t_len, _ = r.shape
    rows = ctx_len
    nb = t_len // rows

    def bwd_block(t):
        return jnp.where(t < 1, 0, nb - t)

    def fwd(col):
        return pl.BlockSpec((1, rows, _GW), lambda b, t: (b, t, col))

    def bwd(col):
        return pl.BlockSpec((1, rows, _GW), lambda b, t: (b, bwd_block(t), col))

    return pl.pallas_call(
        _ret_kernel,
        out_shape=(jax.ShapeDtypeStruct((bsz, t_len, _GW), BF16),
                   jax.ShapeDtypeStruct((bsz, t_len, _GW), BF16)),
        grid=(bsz, nb),
        in_specs=[fwd(0), fwd(1), fwd(2), bwd(0), bwd(1), bwd(2),
                  pl.BlockSpec((N_HEADS, CHUNK, CHUNK), lambda b, t: (0, 0, 0)),
                  pl.BlockSpec((N_HEADS, CHUNK, CHUNK), lambda b, t: (0, 0, 0)),
                  pl.BlockSpec((6, CHUNK, _GW), lambda b, t: (0, 0, 0))],
        out_specs=(pl.BlockSpec((1, rows, _GW), lambda b, t: (b, t, 0)),
                   pl.BlockSpec((1, rows, _GW), lambda b, t: (b, bwd_block(t), 0))),
        scratch_shapes=[pltpu.VMEM((_GW, _GW), F32), pltpu.VMEM((_GW, _GW), F32)],
        compiler_params=pltpu.CompilerParams(
            dimension_semantics=("arbitrary", "arbitrary"), vmem_limit_bytes=VMEM_LIMIT),
        name="retention",
    )(r, r, r, r, r, r, decf, decb, vecs)


def _retention_tables(lg_f, lg_b):
    pos = jnp.arange(CHUNK, dtype=F32)
    diff = pos[:, None] - pos[None, :]
    keep_f = diff >= 0
    dec_f = jnp.where(keep_f, jnp.exp(lg_f[:, None, None] * jnp.where(keep_f, diff, 0.0)), 0.0)
    keep_b = diff < 0
    dec_b = jnp.where(keep_b, jnp.exp(lg_b[:, None, None] * jnp.where(keep_b, -diff, 0.0)), 0.0)

    def lanes(tab):
        return jnp.repeat(tab.T, HEAD_DIM, axis=1)

    qdec_f = lanes(jnp.exp(lg_f[:, None] * (pos + 1.0)))
    kdec_f = lanes(jnp.exp(lg_f[:, None] * (CHUNK - 1.0 - pos)))
    qdec_b = lanes(jnp.exp(lg_b[:, None] * (CHUNK - pos)))
    kdec_b = lanes(jnp.exp(lg_b[:, None] * pos))
    cdec_f = jnp.broadcast_to(jnp.repeat(jnp.exp(lg_f * CHUNK), HEAD_DIM)[None, :], (CHUNK, _GW))
    cdec_b = jnp.broadcast_to(jnp.repeat(jnp.exp(lg_b * CHUNK), HEAD_DIM)[None, :], (CHUNK, _GW))
    vecs = jnp.stack([qdec_f, kdec_f, cdec_f, qdec_b, kdec_b, cdec_b]).astype(F32)
    return dec_f.astype(F32), dec_b.astype(F32), vecs


def _gattn_online_kernel(q_ref, kt_ref, va_ref, gate_ref, o_ref, m_ref, acc_ref, *,
                         tk, n_kt, ctx_qtiles, ctx_kt):
    qi = pl.program_id(2)
    tq = q_ref.shape[1]
    q = q_ref[0]
    q2 = jnp.concatenate([q[:, :HEAD_DIM], q[:, HEAD_DIM:]], axis=0)
    m_ref[...] = jnp.full(m_ref.shape, NEG_INF, F32)
    acc_ref[...] = jnp.zeros_like(acc_ref)

    def body(kt, carry):
        start = pl.multiple_of(kt * tk, tk)
        k = kt_ref[0, 0, :, pl.ds(start, tk)]
        v = va_ref[0, 0, pl.ds(start, tk), :]
        s = jnp.dot(q2, k, preferred_element_type=F32)
        m_prev = m_ref[...]
        m_new = jnp.maximum(m_prev, jnp.max(s, axis=-1, keepdims=True))
        p = jnp.exp(s - m_new).astype(BF16)
        acc_ref[...] = (jnp.exp(m_prev - m_new) * acc_ref[...]
                        + jnp.dot(p, v, preferred_element_type=F32))
        m_ref[...] = m_new
        return carry

    lax.fori_loop(0, jnp.where(qi < ctx_qtiles, ctx_kt, n_kt), body, 0)
    acc = acc_ref[...]
    o = acc[:, :HEAD_DIM] / acc[:, HEAD_DIM:HEAD_DIM + 1]
    out = jnp.concatenate([o[:tq], o[tq:]], axis=1)
    o_ref[0] = (out * gate_ref[0].astype(F32)).astype(o_ref.dtype)


def _global_attention_online(g, *, ctx_len):
    bsz, t_len, _ = g.shape
    tq, tk = ATTN_TQ, ATTN_TK_ONLINE
    k = g[:, :, _GW:_GW + KV_WIDTH].reshape(bsz, t_len, N_KV, HEAD_DIM)
    v = g[:, :, _GW + KV_WIDTH:_GW + 2 * KV_WIDTH].reshape(bsz, t_len, N_KV, HEAD_DIM)
    kt = jnp.transpose(k, (0, 2, 3, 1))
    ones = jnp.ones((bsz, N_KV, t_len, 1), v.dtype)
    zeros = jnp.zeros((bsz, N_KV, t_len, 128 - HEAD_DIM - 1), v.dtype)
    va = jnp.concatenate([jnp.transpose(v, (0, 2, 1, 3)), ones, zeros], axis=-1)
    return pl.pallas_call(
        functools.partial(_gattn_online_kernel, tk=tk, n_kt=t_len // tk,
                          ctx_qtiles=ctx_len // tq, ctx_kt=ctx_len // tk),
        out_shape=jax.ShapeDtypeStruct((bsz, t_len, _GW), BF16),
        grid=(bsz, N_KV, t_len // tq),
        in_specs=[
            pl.BlockSpec((1, tq, 128), lambda b, gi, qi: (b, qi, gi)),
            pl.BlockSpec((1, 1, HEAD_DIM, t_len), lambda b, gi, qi: (b, gi, 0, 0)),
            pl.BlockSpec((1, 1, t_len, 128), lambda b, gi, qi: (b, gi, 0, 0)),
            pl.BlockSpec((1, tq, 128), lambda b, gi, qi: (b, qi, 4 + gi)),
        ],
        out_specs=pl.BlockSpec((1, tq, 128), lambda b, gi, qi: (b, qi, gi)),
        scratch_shapes=[pltpu.VMEM((2 * tq, 1), F32), pltpu.VMEM((2 * tq, 128), F32)],
        compiler_params=pltpu.CompilerParams(
            dimension_semantics=("arbitrary", "arbitrary", "arbitrary"),
            vmem_limit_bytes=VMEM_LIMIT),
        name="global_attention_online",
    )(g, kt, va, g)


def _gattn_kernel(q_ref, k_ref, vat_ref, gate_ref, o_ref, acc_ref, *, tk, n_kt, ctx_qtiles, ctx_len):
    gi = pl.program_id(1)
    qi = pl.program_id(2)
    tq = q_ref.shape[1]
    qt = q_ref[0].T
    top = jnp.concatenate([qt[:HEAD_DIM], qt[HEAD_DIM:]], axis=1)
    zero = jnp.zeros_like(top)
    w = jnp.where(gi == 0, jnp.concatenate([top, zero], axis=0), jnp.concatenate([zero, top], axis=0))

    def tile(start, width):
        k = k_ref[0, pl.ds(start, width), :]
        vat = vat_ref[0, 0, :, pl.ds(start, width)]
        st = jnp.dot(k, w, preferred_element_type=F32)
        acc_ref[...] += jnp.dot(vat, jnp.exp(st).astype(BF16), preferred_element_type=F32)

    acc_ref[...] = jnp.zeros_like(acc_ref)

    @pl.when(qi < ctx_qtiles)
    def _():
        tile(0, ctx_len)

    @pl.when(qi >= ctx_qtiles)
    def _():
        for kt in range(n_kt):
            tile(kt * tk, tk)

    acc = acc_ref[...]
    ot = acc[:HEAD_DIM] / acc[HEAD_DIM:HEAD_DIM + 1]
    out = jnp.concatenate([ot[:, :tq].T, ot[:, tq:].T], axis=1)
    o_ref[0] = (out * gate_ref[0].astype(F32)).astype(o_ref.dtype)


def _global_attention_fast(g, gvt, *, ctx_len):
    bsz, t_len, _ = g.shape
    tq = ATTN_TQ
    tk = next(w for w in ATTN_TK_FAST if t_len % w == 0)
    return pl.pallas_call(
        functools.partial(_gattn_kernel, tk=tk, n_kt=t_len // tk,
                          ctx_qtiles=ctx_len // tq, ctx_len=ctx_len),
        out_shape=jax.ShapeDtypeStruct((bsz, t_len, _GW), BF16),
        grid=(bsz, N_KV, t_len // tq),
        in_specs=[
            pl.BlockSpec((1, tq, 128), lambda b, gi, qi: (b, qi, gi)),
            pl.BlockSpec((1, t_len, KV_WIDTH), lambda b, gi, qi: (b, 0, _GW // KV_WIDTH)),
            pl.BlockSpec((1, 1, VAT_ROWS, t_len), lambda b, gi, qi: (b, gi, 0, 0)),
            pl.BlockSpec((1, tq, 128), lambda b, gi, qi: (b, qi, 4 + gi)),
        ],
        out_specs=pl.BlockSpec((1, tq, 128), lambda b, gi, qi: (b, qi, gi)),
        scratch_shapes=[pltpu.VMEM((VAT_ROWS, 2 * tq), F32)],
        compiler_params=pltpu.CompilerParams(
            dimension_semantics=("arbitrary", "arbitrary", "arbitrary"),
            vmem_limit_bytes=VMEM_LIMIT),
        name="global_attention",
    )(g, g, gvt, g)


def _global_attention(g, gvt, score_bound, *, ctx_len):
    return lax.cond(score_bound <= MAX_UNSHIFTED_SCORE,
                    functools.partial(_global_attention_fast, ctx_len=ctx_len),
                    lambda g_, gvt_: _global_attention_online(g_, ctx_len=ctx_len), g, gvt)


def _wattn_kernel(sink_ref, q_ref, k_ref, vat_ref, gate_ref, o_ref, *,
                  chunks_per_step, ctx_len, t_len):
    n_loc = 3 * CHUNK
    lane_head = lax.broadcasted_iota(jnp.int32, (1, N_HEADS * CHUNK), 1) // CHUNK
    sink = jnp.zeros((1, N_HEADS * CHUNK), F32)
    for hh in range(N_HEADS):
        sink = jnp.where(lane_head == hh, sink_ref[hh], sink)
    k_ctx = k_ref[0, 0:ctx_len, :]
    zero = jnp.zeros((HEAD_DIM, CHUNK), BF16)
    key_row = lax.broadcasted_iota(jnp.int32, (n_loc, CHUNK), 0)
    q_minus_k = lax.broadcasted_iota(jnp.int32, (n_loc, CHUNK), 1) - key_row

    def group(gidx, carry):
        base = gidx * group_size
        rows, starts, st_locs, st_ctxs = [], [], [], []
        for i in range(group_size):
            c = pl.program_id(1) * chunks_per_step + base + i
            rows.append(pl.ds(pl.multiple_of((base + i) * CHUNK, CHUNK), CHUNK))
            qt = q_ref[0, rows[i], :].T
            blk = [qt[hh * HEAD_DIM:(hh + 1) * HEAD_DIM] for hh in range(N_HEADS)]
            w = jnp.concatenate([jnp.concatenate([blk[0], blk[1], zero, zero], axis=1),
                                 jnp.concatenate([zero, zero, blk[2], blk[3]], axis=1)], axis=0)
            start = pl.multiple_of(jnp.clip((c - 1) * CHUNK, 0, t_len - n_loc), CHUNK)
            starts.append(start)
            st_loc = jnp.dot(k_ref[0, pl.ds(start, n_loc), :], w, preferred_element_type=F32)
            st_ctxs.append(jnp.dot(k_ctx, w, preferred_element_type=F32))
            offset = jnp.where(c * CHUNK >= ctx_len, c * CHUNK - start, 4 * WINDOW + n_loc)
            valid = (jnp.abs(q_minus_k + offset) <= WINDOW) & (key_row >= ctx_len - start)
            st_locs.append(jnp.concatenate(
                [jnp.where(valid, st_loc[:, hh * CHUNK:(hh + 1) * CHUNK], NEG_INF)
                 for hh in range(N_HEADS)], axis=1))
        p_locs, p_ctxs, e_sinks = [], [], []
        for st_loc, st_ctx in zip(st_locs, st_ctxs):
            m = jnp.maximum(jnp.maximum(jnp.max(st_loc, axis=0, keepdims=True),
                                        jnp.max(st_ctx, axis=0, keepdims=True)), sink)
            p_locs.append(jnp.exp(st_loc - m).astype(BF16))
            p_ctxs.append(jnp.exp(st_ctx - m).astype(BF16))
            e_sinks.append(jnp.exp(sink - m))
        accs = []
        for start, p_loc, p_ctx in zip(starts, p_locs, p_ctxs):
            for gi in range(N_KV):
                cols = slice(gi * 2 * CHUNK, (gi + 1) * 2 * CHUNK)
                accs.append(jnp.dot(vat_ref[0, gi, :, pl.ds(start, n_loc)], p_loc[:, cols],
                                    preferred_element_type=F32)
                            + jnp.dot(vat_ref[0, gi, :, 0:ctx_len], p_ctx[:, cols],
                                      preferred_element_type=F32))
        for i in range(group_size):
            outs = []
            for gi in range(N_KV):
                cols = slice(gi * 2 * CHUNK, (gi + 1) * 2 * CHUNK)
                acc = accs[i * N_KV + gi]
                ot = acc[:HEAD_DIM] / (acc[HEAD_DIM:HEAD_DIM + 1] + e_sinks[i][:, cols])
                outs += [ot[:, :CHUNK].T, ot[:, CHUNK:].T]
            out = jnp.concatenate(outs, axis=1)
            o_ref[0, rows[i], :] = (out * gate_ref[0, rows[i], :].astype(F32)).astype(o_ref.dtype)
        return carry

    group_size = 5 if chunks_per_step % 5 == 0 else 1
    lax.fori_loop(0, chunks_per_step // group_size, group, 0)


def _window_attention(s, svt, sink, *, ctx_len):
    bsz, t_len, _ = s.shape
    nc = t_len // CHUNK
    cps = next(n for n in WINDOW_CHUNKS_PER_STEP if nc % n == 0)
    rows = cps * CHUNK
    return pl.pallas_call(
        functools.partial(_wattn_kernel, chunks_per_step=cps, ctx_len=ctx_len, t_len=t_len),
        out_shape=jax.ShapeDtypeStruct((bsz, t_len, _GW), BF16),
        grid=(bsz, nc // cps),
        in_specs=[
            pl.BlockSpec(memory_space=pltpu.SMEM),
            pl.BlockSpec((1, rows, _GW), lambda b, j: (b, j, 0)),
            pl.BlockSpec((1, t_len, KV_WIDTH), lambda b, j: (b, 0, _GW // KV_WIDTH)),
            pl.BlockSpec((1, N_KV, VAT_ROWS, t_len), lambda b, j: (b, 0, 0, 0)),
            pl.BlockSpec((1, rows, _GW), lambda b, j: (b, j, 2)),
        ],
        out_specs=pl.BlockSpec((1, rows, _GW), lambda b, j: (b, j, 0)),
        compiler_params=pltpu.CompilerParams(
            dimension_semantics=("arbitrary", "arbitrary"), vmem_limit_bytes=VMEM_LIMIT),
        name="window_attention",
    )(sink, s, s, svt, s)


def _outproj_kernel(x_ref, mod_ref, mlp_ref, yf_ref, yb_ref, rg_ref, ga_ref, sw_ref,
                    rn_ref, bd_ref, w_ref, o_ref, *, ctx_len, d_model):
    b = pl.program_id(0)
    t = pl.program_id(1)
    tm = x_ref.shape[1]
    y = yf_ref[0].astype(F32) + yb_ref[0].astype(F32)
    bd = bd_ref[...]
    ms = _group_mean_sq(y, bd)
    ret = (y * lax.rsqrt(ms + RMS_EPS) * rn_ref[...]) * rg_ref[0].astype(F32)
    acc = jnp.dot(mlp_ref[0], w_ref[0:_GW, :], preferred_element_type=F32)
    acc += jnp.dot(ret.astype(BF16), w_ref[_GW:2 * _GW, :], preferred_element_type=F32)
    acc += jnp.dot(ga_ref[0], w_ref[2 * _GW:3 * _GW, :], preferred_element_type=F32)
    acc += jnp.dot(sw_ref[0], w_ref[3 * _GW:4 * _GW, :], preferred_element_type=F32)
    for c in range(tm // CHUNK):
        row = jnp.where(t * tm + c * CHUNK < ctx_len, 2, b)
        gate = mod_ref[pl.ds(row, 1), :][:, 2 * d_model:]
        rows = slice(c * CHUNK, (c + 1) * CHUNK)
        o_ref[0, rows, :] = x_ref[0, rows, :] + gate * acc[rows]


def _outproj(xs, mod, mlp, yf, yb, r, ga, sw, rn, bd, w_out, *, ctx_len):
    bsz, t_len, d = xs.shape
    tm = next(n for n in OUTPROJ_TILES if t_len % n == 0)
    const2 = lambda b, t: (0, 0)
    tok = lambda b, t: (b, t, 0)
    grp = pl.BlockSpec((1, tm, _GW), tok)
    return pl.pallas_call(
        functools.partial(_outproj_kernel, ctx_len=ctx_len, d_model=d),
        out_shape=jax.ShapeDtypeStruct(xs.shape, xs.dtype),
        grid=(bsz, t_len // tm),
        in_specs=[
            pl.BlockSpec((1, tm, d), tok),
            pl.BlockSpec((8, 3 * d), const2),
            grp, grp, grp,
            pl.BlockSpec((1, tm, _GW), lambda b, t: (b, t, 3)),
            grp, grp,
            pl.BlockSpec((1, _GW), const2),
            pl.BlockSpec((_GW, _GW), const2),
            pl.BlockSpec((4 * _GW, d), const2),
        ],
        out_specs=pl.BlockSpec((1, tm, d), tok),
        input_output_aliases={0: 0},
        compiler_params=pltpu.CompilerParams(
            dimension_semantics=("arbitrary", "arbitrary"), vmem_limit_bytes=VMEM_LIMIT),
        name="outproj",
    )(xs, mod, mlp, yf, yb, r, ga, sw, rn, bd, w_out)


def _rope_tables(seq_len, ctx_len):
    rows = seq_len // GRID_W
    row = jnp.broadcast_to(jnp.arange(rows, dtype=F32)[:, None], (rows, GRID_W)).reshape(-1)
    col = jnp.broadcast_to(jnp.arange(GRID_W, dtype=F32)[None, :], (rows, GRID_W)).reshape(-1)
    half = HEAD_DIM // 2
    inv_freq = 1.0 / (ROPE_BASE ** (jnp.arange(0, half, 2, dtype=F32) / half))
    ang_r = row[:, None] * inv_freq[None, :]
    ang_c = col[:, None] * inv_freq[None, :]
    ang = jnp.concatenate([ang_r, ang_r, ang_c, ang_c], axis=-1)
    cos = jnp.concatenate([jnp.ones((ctx_len, HEAD_DIM), F32), jnp.cos(ang)], axis=0)
    sin = jnp.concatenate([jnp.zeros((ctx_len, HEAD_DIM), F32), jnp.sin(ang)], axis=0)
    first = (jnp.arange(HEAD_DIM) % (half)) < (half // 2)
    sina = jnp.where(first[None, :], -sin, 0.0)
    sinb = jnp.where(first[None, :], 0.0, sin)
    two = lambda a: jnp.concatenate([a, a], axis=-1)
    return two(cos), two(sina), two(sinb)


def kernel(x, c, ctx, c_ctx, norm_gain, w_mod, b_mod, w_in, w_out, mlp_mix, mlp_bias,
           ret_decay_fwd, ret_decay_bwd, ret_norm, attn_q_norm, attn_k_norm,
           swa_q_norm, swa_k_norm, swa_sink):
    bsz, seq_len, d = x.shape
    ctx_len = ctx.shape[1]
    depth = w_in.shape[0]
    assert w_in.shape[2] == IN_WIDTH and w_out.shape[1] == 4 * _GW
    assert ctx_len % ATTN_TQ == 0 and seq_len % ctx_len == 0 and bsz <= 2

    cond = jnp.zeros((8, d), F32).at[:bsz].set(c).at[2].set(c_ctx)
    mod_all = _modulation(cond, w_mod, b_mod)

    cos, sina, sinb = _rope_tables(seq_len, ctx_len)
    lane_group = jnp.arange(_GW) // HEAD_DIM
    bd = ((lane_group[:, None] == lane_group[None, :]).astype(F32) * (1.0 / HEAD_DIM)).astype(BF16)
    w_in_b = w_in.astype(BF16)
    w_out_b = w_out.astype(BF16)
    mix_b = mlp_mix.astype(BF16)
    two = lambda a: jnp.concatenate([a, a], axis=-1)

    xs = jnp.concatenate([ctx, x], axis=1)
    for i in range(depth):
        mod = mod_all[i]
        mbias = jnp.repeat(mlp_bias[i].T, HEAD_DIM, axis=1)
        qkg = jnp.stack([two(attn_q_norm[i]) * ATTN_SCALE, two(attn_k_norm[i]),
                         two(swa_q_norm[i]) * ATTN_SCALE, two(swa_k_norm[i])]).astype(F32)
        mlp, r, g, s, gvt, svt = _inproj(xs, mod, norm_gain[i][None, :], w_in_b[i], mix_b[i], mbias,
                                         bd, cos, sina, sinb, qkg, ctx_len=ctx_len)
        lg_f = -jnp.exp(ret_decay_fwd[i].astype(F32))
        lg_b = -jnp.exp(ret_decay_bwd[i].astype(F32))
        decf, decb, vecs = _retention_tables(lg_f, lg_b)
        yf, yb = _retention(r, decf, decb, vecs, ctx_len=ctx_len)
        score_bound = (HEAD_DIM * ATTN_SCALE * jnp.max(jnp.abs(attn_q_norm[i]))
                       * jnp.max(jnp.abs(attn_k_norm[i])))
        ga = _global_attention(g, gvt, score_bound, ctx_len=ctx_len)
        sw = _window_attention(s, svt, swa_sink[i].astype(F32), ctx_len=ctx_len)
        xs = _outproj(xs, mod, mlp, yf, yb, r, ga, sw, ret_norm[i].reshape(1, _GW), bd,
                      w_out_b[i], ctx_len=ctx_len)
    return xs[:, ctx_len:, :]
```

```python
import functools

import numpy as np
import jax
import jax.numpy as jnp
from jax import lax
from jax.experimental import pallas as pl
from jax.experimental.pallas import tpu as pltpu

F32 = jnp.float32
BF16 = jnp.bfloat16

GRID_W = 64
CHUNK = 128
WINDOW = 128
HEAD_DIM = 64
N_HEADS = 4
GROUP_WIDTH = N_HEADS * HEAD_DIM
N_KV = 2
KV_WIDTH = N_KV * HEAD_DIM
ROPE_BASE = 10000.0
RMS_EPS = 1e-6
ATTN_SCALE = HEAD_DIM ** -0.5
NEG_INF = -1e30

_GW = GROUP_WIDTH
C_AUV, C_AG = 0, 2 * _GW
C_RQ, C_RK, C_RV, C_RG = 3 * _GW, 4 * _GW, 5 * _GW, 6 * _GW
C_GQ = 7 * _GW
C_GK = C_GQ + _GW
C_GV = C_GK + KV_WIDTH
C_GG = C_GV + KV_WIDTH
C_SQ = C_GG + _GW
C_SK = C_SQ + _GW
C_SV = C_SK + KV_WIDTH
C_SG = C_SV + KV_WIDTH
IN_WIDTH = C_SG + _GW
QKV_WIDTH = 2 * _GW + 2 * KV_WIDTH

INPROJ_TILES = (640, 256)
OUTPROJ_TILES = (1280, 256)
ATTN_TQ = 256
ATTN_TK_ONLINE = 256
ATTN_TK_FAST = (3328, 256)
MAX_UNSHIFTED_SCORE = 32.0
WINDOW_CHUNKS_PER_STEP = (10, 2, 1)
VAT_ROWS = 80
VMEM_LIMIT = 56 * 1024 * 1024


def _silu(x):
    return x / (1.0 + jnp.exp(-x))


def _head_mask(width, head):
    lane = lax.broadcasted_iota(jnp.int32, (1, width), 1)
    return (lane // HEAD_DIM) == head


def _group_mean_sq(y, bd):
    sq = y * y
    hi = sq.astype(BF16)
    lo = (sq - hi.astype(F32)).astype(BF16)
    return (jnp.dot(hi, bd, preferred_element_type=F32)
            + jnp.dot(lo, bd, preferred_element_type=F32))


def _mod_kernel(cond_ref, w_ref, b_ref, o_ref):
    a = _silu(cond_ref[...])
    o_ref[0] = jnp.dot(a, w_ref[0], preferred_element_type=F32) + b_ref[0]


def _modulation(cond, w_mod, b_mod):
    depth, d, d3 = w_mod.shape
    nblk = d3 // d
    return pl.pallas_call(
        _mod_kernel,
        out_shape=jax.ShapeDtypeStruct((depth, 8, d3), F32),
        grid=(depth, nblk),
        in_specs=[
            pl.BlockSpec((8, d), lambda i, j: (0, 0)),
            pl.BlockSpec((1, d, d), lambda i, j: (i, 0, j)),
            pl.BlockSpec((1, 1, d), lambda i, j: (i, 0, j)),
        ],
        out_specs=pl.BlockSpec((1, 8, d), lambda i, j: (i, 0, j)),
        compiler_params=pltpu.CompilerParams(
            dimension_semantics=("arbitrary", "arbitrary"), vmem_limit_bytes=VMEM_LIMIT),
        name="modulation",
    )(cond, w_mod, b_mod.reshape(depth, 1, d3))


def _inproj_kernel(x_ref, mod_ref, gain_ref, w_ref, mix_ref, mbias_ref, bd_ref,
                   cos_ref, sina_ref, sinb_ref, qkg_ref,
                   mlp_ref, r_ref, g_ref, s_ref, gvt_ref, svt_ref, *, ctx_len, d_model):
    b = pl.program_id(0)
    t = pl.program_id(1)
    d = d_model
    tm = x_ref.shape[1]
    gain = gain_ref[...]
    hs = []
    for c in range(tm // CHUNK):
        row = jnp.where(t * tm + c * CHUNK < ctx_len, 2, b)
        mod = mod_ref[pl.ds(row, 1), :]
        x = x_ref[0, c * CHUNK:(c + 1) * CHUNK, :]
        ms = jnp.mean(x * x, axis=-1, keepdims=True)
        xn = x * lax.rsqrt(ms + RMS_EPS) * gain
        hs.append((xn * (1.0 + mod[:, d:2 * d]) + mod[:, :d]).astype(BF16))
    h = jnp.concatenate(hs, axis=0)

    def proj(lo, width):
        return jnp.dot(h, w_ref[:, lo:lo + width], preferred_element_type=F32)

    bd = bd_ref[...]
    cos = cos_ref[...]
    sina = sina_ref[...]
    sinb = sinb_ref[...]

    def norm_rope_store(out_ref, q, q_ms, k, k_ms, gi):
        qg = qkg_ref[gi:gi + 1, :]
        kg = qkg_ref[gi + 1:gi + 2, :]
        parts = ((q[:, :128], q_ms[:, :128], qg), (q[:, 128:], q_ms[:, 128:], qg), (k, k_ms, kg))
        for j, (y, ms, gain_row) in enumerate(parts):
            yn = y * lax.rsqrt(ms + RMS_EPS) * gain_row
            out = yn * cos + pltpu.roll(yn, 128 - 16, 1) * sina + pltpu.roll(yn, 16, 1) * sinb
            out_ref[0, :, j * 128:(j + 1) * 128] = out.astype(out_ref.dtype)

    pad_row = lax.broadcasted_iota(jnp.int32, (VAT_ROWS - HEAD_DIM, tm), 0)
    ones_then_zeros = jnp.where(pad_row == 0, 1.0, 0.0).astype(BF16)

    def store_v(out_ref, vt_ref, v):
        out_ref[0, :, 384:512] = v.astype(out_ref.dtype)
        vt = v.T.astype(vt_ref.dtype)
        for gi in range(N_KV):
            vt_ref[0, gi, 0:HEAD_DIM, :] = vt[gi * HEAD_DIM:(gi + 1) * HEAD_DIM]
            vt_ref[0, gi, HEAD_DIM:VAT_ROWS, :] = ones_then_zeros

    uv = jax.nn.gelu(proj(C_AUV, 2 * _GW), approximate=True)
    u = uv[:, :_GW]
    v = uv[:, _GW:].astype(BF16)
    ag = _silu(proj(C_AG, _GW))
    gq = proj(C_GQ, _GW)
    kk = jnp.concatenate([proj(C_GK, KV_WIDTH), proj(C_SK, KV_WIDTH)], axis=1)

    r_ref[0, :, 0:_GW] = proj(C_RQ, _GW).astype(r_ref.dtype)
    r_ref[0, :, _GW:2 * _GW] = (proj(C_RK, _GW) * ATTN_SCALE).astype(r_ref.dtype)
    r_ref[0, :, 2 * _GW:3 * _GW] = proj(C_RV, _GW).astype(r_ref.dtype)
    r_ref[0, :, 3 * _GW:4 * _GW] = _silu(proj(C_RG, _GW)).astype(r_ref.dtype)

    masks = [_head_mask(_GW, hh) for hh in range(N_HEADS)]
    for c in range(tm // CHUNK):
        rows = slice(c * CHUNK, (c + 1) * CHUNK)
        vc = v[rows]
        sv = mbias_ref[...]
        for hh in range(N_HEADS):
            mixed = jnp.dot(mix_ref[hh], vc, preferred_element_type=F32)
            sv = sv + jnp.where(masks[hh], mixed, 0.0)
        mlp_ref[0, rows, :] = (u[rows] * sv * ag[rows]).astype(mlp_ref.dtype)

    gq_ms = _group_mean_sq(gq, bd)
    kk_ms = _group_mean_sq(kk, bd)
    sq = proj(C_SQ, _GW)
    store_v(g_ref, gvt_ref, proj(C_GV, KV_WIDTH))
    g_ref[0, :, 512:768] = _silu(proj(C_GG, _GW)).astype(g_ref.dtype)
    norm_rope_store(g_ref, gq, gq_ms, kk[:, :128], kk_ms[:, :128], 0)
    sq_ms = _group_mean_sq(sq, bd)
    store_v(s_ref, svt_ref, proj(C_SV, KV_WIDTH))
    s_ref[0, :, 512:768] = _silu(proj(C_SG, _GW)).astype(s_ref.dtype)
    norm_rope_store(s_ref, sq, sq_ms, kk[:, 128:], kk_ms[:, 128:], 2)


def _inproj(xs, mod, gain, w_in, mix, mbias, bd, cos, sina, sinb, qkg, *, ctx_len):
    bsz, t_len, d = xs.shape
    tm = next(n for n in INPROJ_TILES if t_len % n == 0)
    nt = t_len // tm
    const2 = lambda b, t: (0, 0)
    tok = lambda b, t: (b, t, 0)
    tab = lambda b, t: (t, 0)
    return pl.pallas_call(
        functools.partial(_inproj_kernel, ctx_len=ctx_len, d_model=d),
        out_shape=(
            jax.ShapeDtypeStruct((bsz, t_len, _GW), BF16),
            jax.ShapeDtypeStruct((bsz, t_len, 4 * _GW), BF16),
            jax.ShapeDtypeStruct((bsz, t_len, QKV_WIDTH), BF16),
            jax.ShapeDtypeStruct((bsz, t_len, QKV_WIDTH), BF16),
            jax.ShapeDtypeStruct((bsz, N_KV, VAT_ROWS, t_len), BF16),
            jax.ShapeDtypeStruct((bsz, N_KV, VAT_ROWS, t_len), BF16),
        ),
        grid=(bsz, nt),
        in_specs=[
            pl.BlockSpec((1, tm, d), tok),
            pl.BlockSpec((8, 3 * d), const2),
            pl.BlockSpec((1, d), const2),
            pl.BlockSpec((d, IN_WIDTH), const2),
            pl.BlockSpec((N_HEADS, CHUNK, CHUNK), lambda b, t: (0, 0, 0)),
            pl.BlockSpec((CHUNK, _GW), const2),
            pl.BlockSpec((_GW, _GW), const2),
            pl.BlockSpec((tm, 128), tab),
            pl.BlockSpec((tm, 128), tab),
            pl.BlockSpec((tm, 128), tab),
            pl.BlockSpec((4, 128), const2),
        ],
        out_specs=(
            pl.BlockSpec((1, tm, _GW), tok),
            pl.BlockSpec((1, tm, 4 * _GW), tok),
            pl.BlockSpec((1, tm, QKV_WIDTH), tok),
            pl.BlockSpec((1, tm, QKV_WIDTH), tok),
            pl.BlockSpec((1, N_KV, VAT_ROWS, tm), lambda b, t: (b, 0, 0, t)),
            pl.BlockSpec((1, N_KV, VAT_ROWS, tm), lambda b, t: (b, 0, 0, t)),
        ),
        compiler_params=pltpu.CompilerParams(
            dimension_semantics=("arbitrary", "arbitrary"), vmem_limit_bytes=VMEM_LIMIT),
        name="inproj",
    )(xs, mod, gain, w_in, mix, mbias, bd, cos, sina, sinb, qkg)


def _ret_kernel(qf_ref, kf_ref, vf_ref, qb_ref, kb_ref, vb_ref,
                decf_ref, decb_ref, vec_ref, yf_ref, yb_ref, sf_ref, sb_ref):
    @pl.when(pl.program_id(1) == 0)
    def _():
        sf_ref[...] = jnp.zeros_like(sf_ref)
        sb_ref[...] = jnp.zeros_like(sb_ref)

    n = qf_ref.shape[1] // CHUNK
    masks = [_head_mask(_GW, hh) for hh in range(N_HEADS)]
    nt = (((1,), (1,)), ((), ()))
    tn = (((0,), (0,)), ((), ()))
    fwd = (qf_ref, kf_ref, vf_ref, decf_ref, 0, yf_ref, sf_ref)
    bwd = (qb_ref, kb_ref, vb_ref, decb_ref, 3, yb_ref, sb_ref)
    work = []
    for i in range(n):
        work.append((fwd, slice(i * CHUNK, (i + 1) * CHUNK)))
        work.append((bwd, slice((n - 1 - i) * CHUNK, (n - i) * CHUNK)))

    scores, kvs = [], []
    for (q_ref, k_ref, v_ref, _, vi, _, _), rows in work:
        q, k, v = q_ref[0, rows, :], k_ref[0, rows, :], v_ref[0, rows, :]
        scores.append([lax.dot_general(jnp.where(m, q, jnp.zeros_like(q)), k, nt,
                                       preferred_element_type=F32) for m in masks])
        kd = (k.astype(F32) * vec_ref[vi + 1]).astype(BF16)
        kvs.append(lax.dot_general(kd, v, tn, preferred_element_type=F32))
    intras = []
    for ((_, _, v_ref, dec_ref, _, _, _), rows), sc in zip(work, scores):
        v = v_ref[0, rows, :]
        y = jnp.zeros((CHUNK, _GW), F32)
        for hh in range(N_HEADS):
            p = (sc[hh] * dec_ref[hh]).astype(BF16)
            y = y + jnp.where(masks[hh], jnp.dot(p, v, preferred_element_type=F32), 0.0)
        intras.append(y)
    row_head = lax.broadcasted_iota(jnp.int32, (_GW, _GW), 0) // HEAD_DIM
    col_head = lax.broadcasted_iota(jnp.int32, (_GW, _GW), 1) // HEAD_DIM
    for ((q_ref, _, _, _, vi, y_ref, state_ref), rows), intra, kv in zip(work, intras, kvs):
        state = state_ref[...]
        inter = jnp.dot(q_ref[0, rows, :], state.astype(BF16), preferred_element_type=F32)
        y_ref[0, rows, :] = (intra + inter * vec_ref[vi]).astype(y_ref.dtype)
        state_ref[...] = state * vec_ref[vi + 2, 0:1, :] + jnp.where(row_head == col_head, kv, 0.0)


def _retention(r, decf, decb, vecs, *, ctx_len):
    bsz, t_len, _ = r.shape
    rows = ctx_len
    nb = t_len // rows

    def bwd_block(t):
        return jnp.where(t < 1, 0, nb - t)

    def fwd(col):
        return pl.BlockSpec((1, rows, _GW), lambda b, t: (b, t, col))

    def bwd(col):
        return pl.BlockSpec((1, rows, _GW), lambda b, t: (b, bwd_block(t), col))

    return pl.pallas_call(
        _ret_kernel,
        out_shape=(jax.ShapeDtypeStruct((bsz, t_len, _GW), BF16),
                   jax.ShapeDtypeStruct((bsz, t_len, _GW), BF16)),
        grid=(bsz, nb),
        in_specs=[fwd(0), fwd(1), fwd(2), bwd(0), bwd(1), bwd(2),
                  pl.BlockSpec((N_HEADS, CHUNK, CHUNK), lambda b, t: (0, 0, 0)),
                  pl.BlockSpec((N_HEADS, CHUNK, CHUNK), lambda b, t: (0, 0, 0)),
                  pl.BlockSpec((6, CHUNK, _GW), lambda b, t: (0, 0, 0))],
        out_specs=(pl.BlockSpec((1, rows, _GW), lambda b, t: (b, t, 0)),
                   pl.BlockSpec((1, rows, _GW), lambda b, t: (b, bwd_block(t), 0))),
        scratch_shapes=[pltpu.VMEM((_GW, _GW), F32), pltpu.VMEM((_GW, _GW), F32)],
        compiler_params=pltpu.CompilerParams(
            dimension_semantics=("arbitrary", "arbitrary"), vmem_limit_bytes=VMEM_LIMIT),
        name="retention",
    )(r, r, r, r, r, r, decf, decb, vecs)


def _retention_tables(lg_f, lg_b):
    pos = jnp.arange(CHUNK, dtype=F32)
    diff = pos[:, None] - pos[None, :]
    keep_f = diff >= 0
    dec_f = jnp.where(keep_f, jnp.exp(lg_f[:, None, None] * jnp.where(keep_f, diff, 0.0)), 0.0)
    keep_b = diff < 0
    dec_b = jnp.where(keep_b, jnp.exp(lg_b[:, None, None] * jnp.where(keep_b, -diff, 0.0)), 0.0)

    def lanes(tab):
        return jnp.repeat(tab.T, HEAD_DIM, axis=1)

    qdec_f = lanes(jnp.exp(lg_f[:, None] * (pos + 1.0)))
    kdec_f = lanes(jnp.exp(lg_f[:, None] * (CHUNK - 1.0 - pos)))
    qdec_b = lanes(jnp.exp(lg_b[:, None] * (CHUNK - pos)))
    kdec_b = lanes(jnp.exp(lg_b[:, None] * pos))
    cdec_f = jnp.broadcast_to(jnp.repeat(jnp.exp(lg_f * CHUNK), HEAD_DIM)[None, :], (CHUNK, _GW))
    cdec_b = jnp.broadcast_to(jnp.repeat(jnp.exp(lg_b * CHUNK), HEAD_DIM)[None, :], (CHUNK, _GW))
    vecs = jnp.stack([qdec_f, kdec_f, cdec_f, qdec_b, kdec_b, cdec_b]).astype(F32)
    return dec_f.astype(F32), dec_b.astype(F32), vecs


def _gattn_online_kernel(q_ref, kt_ref, va_ref, gate_ref, o_ref, m_ref, acc_ref, *,
                         tk, n_kt, ctx_qtiles, ctx_kt):
    qi = pl.program_id(2)
    tq = q_ref.shape[1]
    q = q_ref[0]
    q2 = jnp.concatenate([q[:, :HEAD_DIM], q[:, HEAD_DIM:]], axis=0)
    m_ref[...] = jnp.full(m_ref.shape, NEG_INF, F32)
    acc_ref[...] = jnp.zeros_like(acc_ref)

    def body(kt, carry):
        start = pl.multiple_of(kt * tk, tk)
        k = kt_ref[0, 0, :, pl.ds(start, tk)]
        v = va_ref[0, 0, pl.ds(start, tk), :]
        s = jnp.dot(q2, k, preferred_element_type=F32)
        m_prev = m_ref[...]
        m_new = jnp.maximum(m_prev, jnp.max(s, axis=-1, keepdims=True))
        p = jnp.exp(s - m_new).astype(BF16)
        acc_ref[...] = (jnp.exp(m_prev - m_new) * acc_ref[...]
                        + jnp.dot(p, v, preferred_element_type=F32))
        m_ref[...] = m_new
        return carry

    lax.fori_loop(0, jnp.where(qi < ctx_qtiles, ctx_kt, n_kt), body, 0)
    acc = acc_ref[...]
    o = acc[:, :HEAD_DIM] / acc[:, HEAD_DIM:HEAD_DIM + 1]
    out = jnp.concatenate([o[:tq], o[tq:]], axis=1)
    o_ref[0] = (out * gate_ref[0].astype(F32)).astype(o_ref.dtype)


def _global_attention_online(g, *, ctx_len):
    bsz, t_len, _ = g.shape
    tq, tk = ATTN_TQ, ATTN_TK_ONLINE
    k = g[:, :, _GW:_GW + KV_WIDTH].reshape(bsz, t_len, N_KV, HEAD_DIM)
    v = g[:, :, _GW + KV_WIDTH:_GW + 2 * KV_WIDTH].reshape(bsz, t_len, N_KV, HEAD_DIM)
    kt = jnp.transpose(k, (0, 2, 3, 1))
    ones = jnp.ones((bsz, N_KV, t_len, 1), v.dtype)
    zeros = jnp.zeros((bsz, N_KV, t_len, 128 - HEAD_DIM - 1), v.dtype)
    va = jnp.concatenate([jnp.transpose(v, (0, 2, 1, 3)), ones, zeros], axis=-1)
    return pl.pallas_call(
        functools.partial(_gattn_online_kernel, tk=tk, n_kt=t_len // tk,
                          ctx_qtiles=ctx_len // tq, ctx_kt=ctx_len // tk),
        out_shape=jax.ShapeDtypeStruct((bsz, t_len, _GW), BF16),
        grid=(bsz, N_KV, t_len // tq),
        in_specs=[
            pl.BlockSpec((1, tq, 128), lambda b, gi, qi: (b, qi, gi)),
            pl.BlockSpec((1, 1, HEAD_DIM, t_len), lambda b, gi, qi: (b, gi, 0, 0)),
            pl.BlockSpec((1, 1, t_len, 128), lambda b, gi, qi: (b, gi, 0, 0)),
            pl.BlockSpec((1, tq, 128), lambda b, gi, qi: (b, qi, 4 + gi)),
        ],
        out_specs=pl.BlockSpec((1, tq, 128), lambda b, gi, qi: (b, qi, gi)),
        scratch_shapes=[pltpu.VMEM((2 * tq, 1), F32), pltpu.VMEM((2 * tq, 128), F32)],
        compiler_params=pltpu.CompilerParams(
            dimension_semantics=("arbitrary", "arbitrary", "arbitrary"),
            vmem_limit_bytes=VMEM_LIMIT),
        name="global_attention_online",
    )(g, kt, va, g)


def _gattn_kernel(q_ref, k_ref, vat_ref, gate_ref, o_ref, acc_ref, *, tk, n_kt, ctx_qtiles, ctx_len):
    qi = pl.program_id(1)
    tq = q_ref.shape[1]
    qt = q_ref[0].T
    blk = [qt[hh * HEAD_DIM:(hh + 1) * HEAD_DIM] for hh in range(N_HEADS)]
    zero = jnp.zeros_like(blk[0])
    w = jnp.concatenate([jnp.concatenate([blk[0], blk[1], zero, zero], axis=1),
                         jnp.concatenate([zero, zero, blk[2], blk[3]], axis=1)], axis=0)

    def tile(start, width):
        k = k_ref[0, pl.ds(start, width), :]
        p = jnp.exp(jnp.dot(k, w, preferred_element_type=F32)).astype(BF16)
        for gi in range(N_KV):
            vat = vat_ref[0, gi, :, pl.ds(start, width)]
            acc_ref[gi] += jnp.dot(vat, p[:, gi * 2 * tq:(gi + 1) * 2 * tq],
                                   preferred_element_type=F32)

    acc_ref[...] = jnp.zeros_like(acc_ref)

    @pl.when(qi < ctx_qtiles)
    def _():
        tile(0, ctx_len)

    @pl.when(qi >= ctx_qtiles)
    def _():
        for kt in range(n_kt):
            tile(kt * tk, tk)

    outs = []
    for gi in range(N_KV):
        acc = acc_ref[gi]
        ot = acc[:HEAD_DIM] / acc[HEAD_DIM:HEAD_DIM + 1]
        outs += [ot[:, :tq].T, ot[:, tq:].T]
    out = jnp.concatenate(outs, axis=1)
    o_ref[0] = (out * gate_ref[0].astype(F32)).astype(o_ref.dtype)


def _global_attention_fast(g, gvt, *, ctx_len):
    bsz, t_len, _ = g.shape
    tq = ATTN_TQ
    tk = next(w for w in ATTN_TK_FAST if t_len % w == 0)
    return pl.pallas_call(
        functools.partial(_gattn_kernel, tk=tk, n_kt=t_len // tk,
                          ctx_qtiles=ctx_len // tq, ctx_len=ctx_len),
        out_shape=jax.ShapeDtypeStruct((bsz, t_len, _GW), BF16),
        grid=(bsz, t_len // tq),
        in_specs=[
            pl.BlockSpec((1, tq, _GW), lambda b, qi: (b, qi, 0)),
            pl.BlockSpec((1, t_len, KV_WIDTH), lambda b, qi: (b, 0, _GW // KV_WIDTH)),
            pl.BlockSpec((1, N_KV, VAT_ROWS, t_len), lambda b, qi: (b, 0, 0, 0)),
            pl.BlockSpec((1, tq, _GW), lambda b, qi: (b, qi, 2)),
        ],
        out_specs=pl.BlockSpec((1, tq, _GW), lambda b, qi: (b, qi, 0)),
        scratch_shapes=[pltpu.VMEM((N_KV, VAT_ROWS, 2 * tq), F32)],
        compiler_params=pltpu.CompilerParams(
            dimension_semantics=("arbitrary", "arbitrary"), vmem_limit_bytes=VMEM_LIMIT),
        name="global_attention",
    )(g, g, gvt, g)


def _global_attention(g, gvt, score_bound, *, ctx_len):
    return lax.cond(score_bound <= MAX_UNSHIFTED_SCORE,
                    functools.partial(_global_attention_fast, ctx_len=ctx_len),
                    lambda g_, gvt_: _global_attention_online(g_, ctx_len=ctx_len), g, gvt)


def _wattn_kernel(sink_ref, q_ref, k_ref, vat_ref, gate_ref, o_ref, *,
                  chunks_per_step, ctx_len, t_len):
    n_loc = 3 * CHUNK
    lane_head = lax.broadcasted_iota(jnp.int32, (1, N_HEADS * CHUNK), 1) // CHUNK
    sink = jnp.zeros((1, N_HEADS * CHUNK), F32)
    for hh in range(N_HEADS):
        sink = jnp.where(lane_head == hh, sink_ref[hh], sink)
    k_ctx = k_ref[0, 0:ctx_len, :]
    zero = jnp.zeros((HEAD_DIM, CHUNK), BF16)
    key_row = lax.broadcasted_iota(jnp.int32, (n_loc, CHUNK), 0)
    q_minus_k = lax.broadcasted_iota(jnp.int32, (n_loc, CHUNK), 1) - key_row

    def group(gidx, carry):
        base = gidx * group_size
        rows, starts, st_locs, st_ctxs = [], [], [], []
        for i in range(group_size):
            c = pl.program_id(1) * chunks_per_step + base + i
            rows.append(pl.ds(pl.multiple_of((base + i) * CHUNK, CHUNK), CHUNK))
            qt = q_ref[0, rows[i], :].T
            blk = [qt[hh * HEAD_DIM:(hh + 1) * HEAD_DIM] for hh in range(N_HEADS)]
            w = jnp.concatenate([jnp.concatenate([blk[0], blk[1], zero, zero], axis=1),
                                 jnp.concatenate([zero, zero, blk[2], blk[3]], axis=1)], axis=0)
            start = pl.multiple_of(jnp.clip((c - 1) * CHUNK, 0, t_len - n_loc), CHUNK)
            starts.append(start)
            st_loc = jnp.dot(k_ref[0, pl.ds(start, n_loc), :], w, preferred_element_type=F32)
            st_ctxs.append(jnp.dot(k_ctx, w, preferred_element_type=F32))
            offset = jnp.where(c * CHUNK >= ctx_len, c * CHUNK - start, 4 * WINDOW + n_loc)
            valid = (jnp.abs(q_minus_k + offset) <= WINDOW) & (key_row >= ctx_len - start)
            st_locs.append(jnp.concatenate(
                [jnp.where(valid, st_loc[:, hh * CHUNK:(hh + 1) * CHUNK], NEG_INF)
                 for hh in range(N_HEADS)], axis=1))
        p_locs, p_ctxs, e_sinks = [], [], []
        for st_loc, st_ctx in zip(st_locs, st_ctxs):
            m = jnp.maximum(jnp.maximum(jnp.max(st_loc, axis=0, keepdims=True),
                                        jnp.max(st_ctx, axis=0, keepdims=True)), sink)
            p_locs.append(jnp.exp(st_loc - m).astype(BF16))
            p_ctxs.append(jnp.exp(st_ctx - m).astype(BF16))
            e_sinks.append(jnp.exp(sink - m))
        accs = []
        for start, p_loc, p_ctx in zip(starts, p_locs, p_ctxs):
            for gi in range(N_KV):
                cols = slice(gi * 2 * CHUNK, (gi + 1) * 2 * CHUNK)
                accs.append(jnp.dot(vat_ref[0, gi, :, pl.ds(start, n_loc)], p_loc[:, cols],
                                    preferred_element_type=F32)
                            + jnp.dot(vat_ref[0, gi, :, 0:ctx_len], p_ctx[:, cols],
                                      preferred_element_type=F32))
        for i in range(group_size):
            outs = []
            for gi in range(N_KV):
                cols = slice(gi * 2 * CHUNK, (gi + 1) * 2 * CHUNK)
                acc = accs[i * N_KV + gi]
                ot = acc[:HEAD_DIM] / (acc[HEAD_DIM:HEAD_DIM + 1] + e_sinks[i][:, cols])
                outs += [ot[:, :CHUNK].T, ot[:, CHUNK:].T]
            out = jnp.concatenate(outs, axis=1)
            o_ref[0, rows[i], :] = (out * gate_ref[0, rows[i], :].astype(F32)).astype(o_ref.dtype)
        return carry

    group_size = 5 if chunks_per_step % 5 == 0 else 1
    lax.fori_loop(0, chunks_per_step // group_size, group, 0)


def _window_attention(s, svt, sink, *, ctx_len):
    bsz, t_len, _ = s.shape
    nc = t_len // CHUNK
    cps = next(n for n in WINDOW_CHUNKS_PER_STEP if nc % n == 0)
    rows = cps * CHUNK
    return pl.pallas_call(
        functools.partial(_wattn_kernel, chunks_per_step=cps, ctx_len=ctx_len, t_len=t_len),
        out_shape=jax.ShapeDtypeStruct((bsz, t_len, _GW), BF16),
        grid=(bsz, nc // cps),
        in_specs=[
            pl.BlockSpec(memory_space=pltpu.SMEM),
            pl.BlockSpec((1, rows, _GW), lambda b, j: (b, j, 0)),
            pl.BlockSpec((1, t_len, KV_WIDTH), lambda b, j: (b, 0, _GW // KV_WIDTH)),
            pl.BlockSpec((1, N_KV, VAT_ROWS, t_len), lambda b, j: (b, 0, 0, 0)),
            pl.BlockSpec((1, rows, _GW), lambda b, j: (b, j, 2)),
        ],
        out_specs=pl.BlockSpec((1, rows, _GW), lambda b, j: (b, j, 0)),
        compiler_params=pltpu.CompilerParams(
            dimension_semantics=("arbitrary", "arbitrary"), vmem_limit_bytes=VMEM_LIMIT),
        name="window_attention",
    )(sink, s, s, svt, s)


def _outproj_kernel(x_ref, mod_ref, mlp_ref, yf_ref, yb_ref, rg_ref, ga_ref, sw_ref,
                    rn_ref, bd_ref, w_ref, o_ref, *, ctx_len, d_model):
    b = pl.program_id(0)
    t = pl.program_id(1)
    tm = x_ref.shape[1]
    y = yf_ref[0].astype(F32) + yb_ref[0].astype(F32)
    bd = bd_ref[...]
    ms = _group_mean_sq(y, bd)
    ret = (y * lax.rsqrt(ms + RMS_EPS) * rn_ref[...]) * rg_ref[0].astype(F32)
    acc = jnp.dot(mlp_ref[0], w_ref[0:_GW, :], preferred_element_type=F32)
    acc += jnp.dot(ret.astype(BF16), w_ref[_GW:2 * _GW, :], preferred_element_type=F32)
    acc += jnp.dot(ga_ref[0], w_ref[2 * _GW:3 * _GW, :], preferred_element_type=F32)
    acc += jnp.dot(sw_ref[0], w_ref[3 * _GW:4 * _GW, :], preferred_element_type=F32)
    for c in range(tm // CHUNK):
        row = jnp.where(t * tm + c * CHUNK < ctx_len, 2, b)
        gate = mod_ref[pl.ds(row, 1), :][:, 2 * d_model:]
        rows = slice(c * CHUNK, (c + 1) * CHUNK)
        o_ref[0, rows, :] = x_ref[0, rows, :] + gate * acc[rows]


def _outproj(xs, mod, mlp, yf, yb, r, ga, sw, rn, bd, w_out, *, ctx_len):
    bsz, t_len, d = xs.shape
    tm = next(n for n in OUTPROJ_TILES if t_len % n == 0)
    const2 = lambda b, t: (0, 0)
    tok = lambda b, t: (b, t, 0)
    grp = pl.BlockSpec((1, tm, _GW), tok)
    return pl.pallas_call(
        functools.partial(_outproj_kernel, ctx_len=ctx_len, d_model=d),
        out_shape=jax.ShapeDtypeStruct(xs.shape, xs.dtype),
        grid=(bsz, t_len // tm),
        in_specs=[
            pl.BlockSpec((1, tm, d), tok),
            pl.BlockSpec((8, 3 * d), const2),
            grp, grp, grp,
            pl.BlockSpec((1, tm, _GW), lambda b, t: (b, t, 3)),
            grp, grp,
            pl.BlockSpec((1, _GW), const2),
            pl.BlockSpec((_GW, _GW), const2),
            pl.BlockSpec((4 * _GW, d), const2),
        ],
        out_specs=pl.BlockSpec((1, tm, d), tok),
        input_output_aliases={0: 0},
        compiler_params=pltpu.CompilerParams(
            dimension_semantics=("arbitrary", "arbitrary"), vmem_limit_bytes=VMEM_LIMIT),
        name="outproj",
    )(xs, mod, mlp, yf, yb, r, ga, sw, rn, bd, w_out)


def _rope_tables(seq_len, ctx_len):
    rows = seq_len // GRID_W
    half = HEAD_DIM // 2
    inv_freq = 1.0 / (ROPE_BASE ** (jnp.arange(0, half, 2, dtype=F32) / half))
    ang_r = jnp.arange(rows, dtype=F32)[:, None] * inv_freq[None, :]
    ang_c = jnp.arange(GRID_W, dtype=F32)[:, None] * inv_freq[None, :]

    def table(fn):
        by_row = jnp.repeat(fn(ang_r), GRID_W, axis=0)
        by_col = jnp.tile(fn(ang_c), (rows, 1))
        return jnp.concatenate([by_row, by_row, by_col, by_col], axis=-1)

    cos = jnp.concatenate([jnp.ones((ctx_len, HEAD_DIM), F32), table(jnp.cos)], axis=0)
    sin = jnp.concatenate([jnp.zeros((ctx_len, HEAD_DIM), F32), table(jnp.sin)], axis=0)
    first = (jnp.arange(HEAD_DIM) % (half)) < (half // 2)
    sina = jnp.where(first[None, :], -sin, 0.0)
    sinb = jnp.where(first[None, :], 0.0, sin)
    two = lambda a: jnp.concatenate([a, a], axis=-1)
    return two(cos), two(sina), two(sinb)


def kernel(x, c, ctx, c_ctx, norm_gain, w_mod, b_mod, w_in, w_out, mlp_mix, mlp_bias,
           ret_decay_fwd, ret_decay_bwd, ret_norm, attn_q_norm, attn_k_norm,
           swa_q_norm, swa_k_norm, swa_sink):
    bsz, seq_len, d = x.shape
    ctx_len = ctx.shape[1]
    depth = w_in.shape[0]
    assert w_in.shape[2] == IN_WIDTH and w_out.shape[1] == 4 * _GW
    assert ctx_len % ATTN_TQ == 0 and seq_len % ctx_len == 0 and bsz <= 2

    cond = jnp.zeros((8, d), F32).at[:bsz].set(c).at[2].set(c_ctx)
    mod_all = _modulation(cond, w_mod, b_mod)

    cos, sina, sinb = _rope_tables(seq_len, ctx_len)
    lane_group = jnp.arange(_GW) // HEAD_DIM
    bd = ((lane_group[:, None] == lane_group[None, :]).astype(F32) * (1.0 / HEAD_DIM)).astype(BF16)
    w_in_b = w_in.astype(BF16)
    w_out_b = w_out.astype(BF16)
    mix_b = mlp_mix.astype(BF16)
    two = lambda a: jnp.concatenate([a, a], axis=-1)

    xs = jnp.concatenate([ctx, x], axis=1)
    for i in range(depth):
        mod = mod_all[i]
        mbias = jnp.repeat(mlp_bias[i].T, HEAD_DIM, axis=1)
        qkg = jnp.stack([two(attn_q_norm[i]) * ATTN_SCALE, two(attn_k_norm[i]),
                         two(swa_q_norm[i]) * ATTN_SCALE, two(swa_k_norm[i])]).astype(F32)
        mlp, r, g, s, gvt, svt = _inproj(xs, mod, norm_gain[i][None, :], w_in_b[i], mix_b[i], mbias,
                                         bd, cos, sina, sinb, qkg, ctx_len=ctx_len)
        lg_f = -jnp.exp(ret_decay_fwd[i].astype(F32))
        lg_b = -jnp.exp(ret_decay_bwd[i].astype(F32))
        decf, decb, vecs = _retention_tables(lg_f, lg_b)
        yf, yb = _retention(r, decf, decb, vecs, ctx_len=ctx_len)
        score_bound = (HEAD_DIM * ATTN_SCALE * jnp.max(jnp.abs(attn_q_norm[i]))
                       * jnp.max(jnp.abs(attn_k_norm[i])))
        ga = _global_attention(g, gvt, score_bound, ctx_len=ctx_len)
        sw = _window_attention(s, svt, swa_sink[i].astype(F32), ctx_len=ctx_len)
        xs = _outproj(xs, mod, mlp, yf, yb, r, ga, sw, ret_norm[i].reshape(1, _GW), bd,
                      w_out_b[i], ctx_len=ctx_len)
    return xs[:, ctx_len:, :]
```

```python
import functools

import numpy as np
import jax
import jax.numpy as jnp
from jax import lax
from jax.experimental import pallas as pl
from jax.experimental.pallas import tpu as pltpu

F32 = jnp.float32
BF16 = jnp.bfloat16

GRID_W = 64
CHUNK = 128
WINDOW = 128
HEAD_DIM = 64
N_HEADS = 4
GROUP_WIDTH = N_HEADS * HEAD_DIM
N_KV = 2
KV_WIDTH = N_KV * HEAD_DIM
ROPE_BASE = 10000.0
RMS_EPS = 1e-6
ATTN_SCALE = HEAD_DIM ** -0.5
NEG_INF = -1e30

_GW = GROUP_WIDTH
C_AUV, C_AG = 0, 2 * _GW
C_RQ, C_RK, C_RV, C_RG = 3 * _GW, 4 * _GW, 5 * _GW, 6 * _GW
C_GQ = 7 * _GW
C_GK = C_GQ + _GW
C_GV = C_GK + KV_WIDTH
C_GG = C_GV + KV_WIDTH
C_SQ = C_GG + _GW
C_SK = C_SQ + _GW
C_SV = C_SK + KV_WIDTH
C_SG = C_SV + KV_WIDTH
IN_WIDTH = C_SG + _GW
QKV_WIDTH = 2 * _GW + 2 * KV_WIDTH

INPROJ_TILES = (640, 256)
OUTPROJ_TILES = (1280, 256)
ATTN_TQ = 256
ATTN_TQ_FAST = 512
ATTN_TK_ONLINE = 256
ATTN_TK_FAST = (1664, 256)
MAX_UNSHIFTED_SCORE = 32.0
WINDOW_CHUNKS_PER_STEP = (10, 2, 1)
VAT_ROWS = 80
VMEM_LIMIT = 56 * 1024 * 1024


def _silu(x):
    return x / (1.0 + jnp.exp(-x))


def _head_mask(width, head):
    lane = lax.broadcasted_iota(jnp.int32, (1, width), 1)
    return (lane // HEAD_DIM) == head


def _group_mean_sq(y, bd):
    sq = y * y
    hi = sq.astype(BF16)
    lo = (sq - hi.astype(F32)).astype(BF16)
    return (jnp.dot(hi, bd, preferred_element_type=F32)
            + jnp.dot(lo, bd, preferred_element_type=F32))


def _mod_kernel(cond_ref, w_ref, b_ref, o_ref):
    a = _silu(cond_ref[...])
    o_ref[0] = jnp.dot(a, w_ref[0], preferred_element_type=F32) + b_ref[0]


def _modulation(cond, w_mod, b_mod):
    depth, d, d3 = w_mod.shape
    nblk = d3 // d
    return pl.pallas_call(
        _mod_kernel,
        out_shape=jax.ShapeDtypeStruct((depth, 8, d3), F32),
        grid=(depth, nblk),
        in_specs=[
            pl.BlockSpec((8, d), lambda i, j: (0, 0)),
            pl.BlockSpec((1, d, d), lambda i, j: (i, 0, j)),
            pl.BlockSpec((1, 1, d), lambda i, j: (i, 0, j)),
        ],
        out_specs=pl.BlockSpec((1, 8, d), lambda i, j: (i, 0, j)),
        compiler_params=pltpu.CompilerParams(
            dimension_semantics=("arbitrary", "arbitrary"), vmem_limit_bytes=VMEM_LIMIT),
        name="modulation",
    )(cond, w_mod, b_mod.reshape(depth, 1, d3))


def _stream_chunk(x_ref, ctx_ref, tile, c, seq_len):
    first_row = tile * x_ref.shape[1] + c * CHUNK
    is_ctx = first_row >= seq_len
    ctx_chunk = jnp.clip((first_row - seq_len) // CHUNK, 0, ctx_ref.shape[1] // CHUNK - 1)
    x_ctx = ctx_ref[0, pl.ds(pl.multiple_of(ctx_chunk * CHUNK, CHUNK), CHUNK), :]
    return jnp.where(is_ctx, x_ctx, x_ref[0, c * CHUNK:(c + 1) * CHUNK, :]), is_ctx


def _inproj_kernel(x_ref, ctx_ref, mod_ref, gain_ref, w_ref, mix_ref, mbias_ref, bd_ref,
                   cos_ref, sina_ref, sinb_ref, qkg_ref,
                   mlp_ref, r_ref, g_ref, s_ref, gvt_ref, svt_ref, *, seq_len, d_model):
    b = pl.program_id(0)
    t = pl.program_id(1)
    d = d_model
    tm = x_ref.shape[1]
    gain = gain_ref[...]
    hs = []
    for c in range(tm // CHUNK):
        x, is_ctx = _stream_chunk(x_ref, ctx_ref, t, c, seq_len)
        mod = mod_ref[pl.ds(jnp.where(is_ctx, 2, b), 1), :]
        ms = jnp.mean(x * x, axis=-1, keepdims=True)
        xn = x * lax.rsqrt(ms + RMS_EPS) * gain
        hs.append((xn * (1.0 + mod[:, d:2 * d]) + mod[:, :d]).astype(BF16))
    h = jnp.concatenate(hs, axis=0)

    def proj(lo, width):
        return jnp.dot(h, w_ref[:, lo:lo + width], preferred_element_type=F32)

    bd = bd_ref[...]
    cos = cos_ref[...]
    sina = sina_ref[...]
    sinb = sinb_ref[...]

    def norm_rope_store(out_ref, q, q_ms, k, k_ms, gi):
        qg = qkg_ref[gi:gi + 1, :]
        kg = qkg_ref[gi + 1:gi + 2, :]
        parts = ((q[:, :128], q_ms[:, :128], qg), (q[:, 128:], q_ms[:, 128:], qg), (k, k_ms, kg))
        for j, (y, ms, gain_row) in enumerate(parts):
            yn = y * lax.rsqrt(ms + RMS_EPS) * gain_row
            out = yn * cos + pltpu.roll(yn, 128 - 16, 1) * sina + pltpu.roll(yn, 16, 1) * sinb
            out_ref[0, :, j * 128:(j + 1) * 128] = out.astype(out_ref.dtype)

    pad_row = lax.broadcasted_iota(jnp.int32, (VAT_ROWS - HEAD_DIM, tm), 0)
    ones_then_zeros = jnp.where(pad_row == 0, 1.0, 0.0).astype(BF16)

    def store_v(out_ref, vt_ref, v):
        out_ref[0, :, 384:512] = v.astype(out_ref.dtype)
        vt = v.T.astype(vt_ref.dtype)
        for gi in range(N_KV):
            vt_ref[0, gi, 0:HEAD_DIM, :] = vt[gi * HEAD_DIM:(gi + 1) * HEAD_DIM]
            vt_ref[0, gi, HEAD_DIM:VAT_ROWS, :] = ones_then_zeros

    uv = jax.nn.gelu(proj(C_AUV, 2 * _GW), approximate=True)
    u = uv[:, :_GW]
    v = uv[:, _GW:].astype(BF16)
    ag = _silu(proj(C_AG, _GW))
    gq = proj(C_GQ, _GW)
    kk = jnp.concatenate([proj(C_GK, KV_WIDTH), proj(C_SK, KV_WIDTH)], axis=1)

    r_ref[0, :, 0:_GW] = proj(C_RQ, _GW).astype(r_ref.dtype)
    r_ref[0, :, _GW:2 * _GW] = (proj(C_RK, _GW) * ATTN_SCALE).astype(r_ref.dtype)
    r_ref[0, :, 2 * _GW:3 * _GW] = proj(C_RV, _GW).astype(r_ref.dtype)
    r_ref[0, :, 3 * _GW:4 * _GW] = _silu(proj(C_RG, _GW)).astype(r_ref.dtype)

    masks = [_head_mask(_GW, hh) for hh in range(N_HEADS)]
    for c in range(tm // CHUNK):
        rows = slice(c * CHUNK, (c + 1) * CHUNK)
        vc = v[rows]
        sv = mbias_ref[...]
        for hh in range(N_HEADS):
            mixed = jnp.dot(mix_ref[hh], vc, preferred_element_type=F32)
            sv = sv + jnp.where(masks[hh], mixed, 0.0)
        mlp_ref[0, rows, :] = (u[rows] * sv * ag[rows]).astype(mlp_ref.dtype)

    gq_ms = _group_mean_sq(gq, bd)
    kk_ms = _group_mean_sq(kk, bd)
    sq = proj(C_SQ, _GW)
    store_v(g_ref, gvt_ref, proj(C_GV, KV_WIDTH))
    g_ref[0, :, 512:768] = _silu(proj(C_GG, _GW)).astype(g_ref.dtype)
    norm_rope_store(g_ref, gq, gq_ms, kk[:, :128], kk_ms[:, :128], 0)
    sq_ms = _group_mean_sq(sq, bd)
    store_v(s_ref, svt_ref, proj(C_SV, KV_WIDTH))
    s_ref[0, :, 512:768] = _silu(proj(C_SG, _GW)).astype(s_ref.dtype)
    norm_rope_store(s_ref, sq, sq_ms, kk[:, 128:], kk_ms[:, 128:], 2)


def _stream_sources(x, ctx, xs):
    if xs is None:
        return x, ctx
    return xs, xs[:, x.shape[1]:, :]


def _inproj(x, ctx, xs, mod, gain, w_in, mix, mbias, bd, cos, sina, sinb, qkg):
    bsz, seq_len, d = x.shape
    ctx_len = ctx.shape[1]
    t_len = seq_len + ctx_len
    tm = next(n for n in INPROJ_TILES if t_len % n == 0)
    nt = t_len // tm
    const2 = lambda b, t: (0, 0)
    tok = lambda b, t: (b, t, 0)
    tab = lambda b, t: (t, 0)
    lat_src, ctx_src = _stream_sources(x, ctx, xs)
    return pl.pallas_call(
        functools.partial(_inproj_kernel, seq_len=seq_len, d_model=d),
        out_shape=(
            jax.ShapeDtypeStruct((bsz, t_len, _GW), BF16),
            jax.ShapeDtypeStruct((bsz, t_len, 4 * _GW), BF16),
            jax.ShapeDtypeStruct((bsz, t_len, QKV_WIDTH), BF16),
            jax.ShapeDtypeStruct((bsz, t_len, QKV_WIDTH), BF16),
            jax.ShapeDtypeStruct((bsz, N_KV, VAT_ROWS, t_len), BF16),
            jax.ShapeDtypeStruct((bsz, N_KV, VAT_ROWS, t_len), BF16),
        ),
        grid=(bsz, nt),
        in_specs=[
            pl.BlockSpec((1, tm, d), lambda b, t: (b, jnp.minimum(t, (lat_src.shape[1] - 1) // tm), 0)),
            pl.BlockSpec((1, ctx_len, d), lambda b, t: (b, 0, 0)),
            pl.BlockSpec((8, 3 * d), const2),
            pl.BlockSpec((1, d), const2),
            pl.BlockSpec((d, IN_WIDTH), const2),
            pl.BlockSpec((N_HEADS, CHUNK, CHUNK), lambda b, t: (0, 0, 0)),
            pl.BlockSpec((CHUNK, _GW), const2),
            pl.BlockSpec((_GW, _GW), const2),
            pl.BlockSpec((tm, 128), tab),
            pl.BlockSpec((tm, 128), tab),
            pl.BlockSpec((tm, 128), tab),
            pl.BlockSpec((4, 128), const2),
        ],
        out_specs=(
            pl.BlockSpec((1, tm, _GW), tok),
            pl.BlockSpec((1, tm, 4 * _GW), tok),
            pl.BlockSpec((1, tm, QKV_WIDTH), tok),
            pl.BlockSpec((1, tm, QKV_WIDTH), tok),
            pl.BlockSpec((1, N_KV, VAT_ROWS, tm), lambda b, t: (b, 0, 0, t)),
            pl.BlockSpec((1, N_KV, VAT_ROWS, tm), lambda b, t: (b, 0, 0, t)),
        ),
        compiler_params=pltpu.CompilerParams(
            dimension_semantics=("arbitrary", "arbitrary"), vmem_limit_bytes=VMEM_LIMIT),
        name="inproj",
    )(lat_src, ctx_src, mod, gain, w_in, mix, mbias, bd, cos, sina, sinb, qkg)


def _ret_kernel(qf_ref, kf_ref, vf_ref, qb_ref, kb_ref, vb_ref,
                decf_ref, decb_ref, vec_ref, yf_ref, yb_ref, sf_ref, sb_ref):
    @pl.when(pl.program_id(1) == 0)
    def _():
        sf_ref[...] = jnp.zeros_like(sf_ref)
        sb_ref[...] = jnp.zeros_like(sb_ref)

    n = qf_ref.shape[1] // CHUNK
    masks = [_head_mask(_GW, hh) for hh in range(N_HEADS)]
    nt = (((1,), (1,)), ((), ()))
    tn = (((0,), (0,)), ((), ()))
    fwd = (qf_ref, kf_ref, vf_ref, decf_ref, 0, yf_ref, sf_ref)
    bwd = (qb_ref, kb_ref, vb_ref, decb_ref, 3, yb_ref, sb_ref)
    work = []
    for i in range(n):
        work.append((fwd, slice(i * CHUNK, (i + 1) * CHUNK)))
        work.append((bwd, slice((n - 1 - i) * CHUNK, (n - i) * CHUNK)))

    scores, kvs = [], []
    for (q_ref, k_ref, v_ref, _, vi, _, _), rows in work:
        q, k, v = q_ref[0, rows, :], k_ref[0, rows, :], v_ref[0, rows, :]
        scores.append([lax.dot_general(jnp.where(m, q, jnp.zeros_like(q)), k, nt,
                                       preferred_element_type=F32) for m in masks])
        kd = (k.astype(F32) * vec_ref[vi + 1]).astype(BF16)
        kvs.append(lax.dot_general(kd, v, tn, preferred_element_type=F32))
    intras = []
    for ((_, _, v_ref, dec_ref, _, _, _), rows), sc in zip(work, scores):
        v = v_ref[0, rows, :]
        y = jnp.zeros((CHUNK, _GW), F32)
        for hh in range(N_HEADS):
            p = (sc[hh] * dec_ref[hh]).astype(BF16)
            y = y + jnp.where(masks[hh], jnp.dot(p, v, preferred_element_type=F32), 0.0)
        intras.append(y)
    row_head = lax.broadcasted_iota(jnp.int32, (_GW, _GW), 0) // HEAD_DIM
    col_head = lax.broadcasted_iota(jnp.int32, (_GW, _GW), 1) // HEAD_DIM
    for ((q_ref, _, _, _, vi, y_ref, state_ref), rows), intra, kv in zip(work, intras, kvs):
        state = state_ref[...]
        inter = jnp.dot(q_ref[0, rows, :], state.astype(BF16), preferred_element_type=F32)
        y_ref[0, rows, :] = (intra + inter * vec_ref[vi]).astype(y_ref.dtype)
        state_ref[...] = state * vec_ref[vi + 2, 0:1, :] + jnp.where(row_head == col_head, kv, 0.0)


def _retention(r, decf, decb, vecs, *, ctx_len):
    bsz, t_len, _ = r.shape
    rows = ctx_len
    nb = t_len // rows

    def fwd_block(t):
        return jnp.where(t < 1, nb - 1, t - 1)

    def bwd_block(t):
        return jnp.where(t < 1, nb - 1, nb - 1 - t)

    def fwd(col):
        return pl.BlockSpec((1, rows, _GW), lambda b, t: (b, fwd_block(t), col))

    def bwd(col):
        return pl.BlockSpec((1, rows, _GW), lambda b, t: (b, bwd_block(t), col))

    return pl.pallas_call(
        _ret_kernel,
        out_shape=(jax.ShapeDtypeStruct((bsz, t_len, _GW), BF16),
                   jax.ShapeDtypeStruct((bsz, t_len, _GW), BF16)),
        grid=(bsz, nb),
        in_specs=[fwd(0), fwd(1), fwd(2), bwd(0), bwd(1), bwd(2),
                  pl.BlockSpec((N_HEADS, CHUNK, CHUNK), lambda b, t: (0, 0, 0)),
                  pl.BlockSpec((N_HEADS, CHUNK, CHUNK), lambda b, t: (0, 0, 0)),
                  pl.BlockSpec((6, CHUNK, _GW), lambda b, t: (0, 0, 0))],
        out_specs=(pl.BlockSpec((1, rows, _GW), lambda b, t: (b, fwd_block(t), 0)),
                   pl.BlockSpec((1, rows, _GW), lambda b, t: (b, bwd_block(t), 0))),
        scratch_shapes=[pltpu.VMEM((_GW, _GW), F32), pltpu.VMEM((_GW, _GW), F32)],
        compiler_params=pltpu.CompilerParams(
            dimension_semantics=("arbitrary", "arbitrary"), vmem_limit_bytes=VMEM_LIMIT),
        name="retention",
    )(r, r, r, r, r, r, decf, decb, vecs)


def _retention_tables(lg_f, lg_b):
    pos = jnp.arange(CHUNK, dtype=F32)
    diff = pos[:, None] - pos[None, :]
    keep_f = diff >= 0
    dec_f = jnp.where(keep_f, jnp.exp(lg_f[:, None, None] * jnp.where(keep_f, diff, 0.0)), 0.0)
    keep_b = diff < 0
    dec_b = jnp.where(keep_b, jnp.exp(lg_b[:, None, None] * jnp.where(keep_b, -diff, 0.0)), 0.0)

    def lanes(tab):
        return jnp.repeat(tab.T, HEAD_DIM, axis=1)

    qdec_f = lanes(jnp.exp(lg_f[:, None] * (pos + 1.0)))
    kdec_f = lanes(jnp.exp(lg_f[:, None] * (CHUNK - 1.0 - pos)))
    qdec_b = lanes(jnp.exp(lg_b[:, None] * (CHUNK - pos)))
    kdec_b = lanes(jnp.exp(lg_b[:, None] * pos))
    cdec_f = jnp.broadcast_to(jnp.repeat(jnp.exp(lg_f * CHUNK), HEAD_DIM)[None, :], (CHUNK, _GW))
    cdec_b = jnp.broadcast_to(jnp.repeat(jnp.exp(lg_b * CHUNK), HEAD_DIM)[None, :], (CHUNK, _GW))
    vecs = jnp.stack([qdec_f, kdec_f, cdec_f, qdec_b, kdec_b, cdec_b]).astype(F32)
    return dec_f.astype(F32), dec_b.astype(F32), vecs


def _gattn_online_kernel(q_ref, kt_ref, va_ref, gate_ref, o_ref, m_ref, acc_ref, *,
                         tk, n_kt, ctx_qtiles, ctx_kt):
    qi = pl.program_id(2)
    tq = q_ref.shape[1]
    q = q_ref[0]
    q2 = jnp.concatenate([q[:, :HEAD_DIM], q[:, HEAD_DIM:]], axis=0)
    m_ref[...] = jnp.full(m_ref.shape, NEG_INF, F32)
    acc_ref[...] = jnp.zeros_like(acc_ref)

    def body(kt, carry):
        start = pl.multiple_of(kt * tk, tk)
        k = kt_ref[0, 0, :, pl.ds(start, tk)]
        v = va_ref[0, 0, pl.ds(start, tk), :]
        s = jnp.dot(q2, k, preferred_element_type=F32)
        m_prev = m_ref[...]
        m_new = jnp.maximum(m_prev, jnp.max(s, axis=-1, keepdims=True))
        p = jnp.exp(s - m_new).astype(BF16)
        acc_ref[...] = (jnp.exp(m_prev - m_new) * acc_ref[...]
                        + jnp.dot(p, v, preferred_element_type=F32))
        m_ref[...] = m_new
        return carry

    is_ctx = qi >= pl.num_programs(2) - ctx_qtiles
    lax.fori_loop(jnp.where(is_ctx, n_kt - ctx_kt, 0), n_kt, body, 0)
    acc = acc_ref[...]
    o = acc[:, :HEAD_DIM] / acc[:, HEAD_DIM:HEAD_DIM + 1]
    out = jnp.concatenate([o[:tq], o[tq:]], axis=1)
    o_ref[0] = (out * gate_ref[0].astype(F32)).astype(o_ref.dtype)


def _global_attention_online(g, *, ctx_len):
    bsz, t_len, _ = g.shape
    tq, tk = ATTN_TQ, ATTN_TK_ONLINE
    k = g[:, :, _GW:_GW + KV_WIDTH].reshape(bsz, t_len, N_KV, HEAD_DIM)
    v = g[:, :, _GW + KV_WIDTH:_GW + 2 * KV_WIDTH].reshape(bsz, t_len, N_KV, HEAD_DIM)
    kt = jnp.transpose(k, (0, 2, 3, 1))
    ones = jnp.ones((bsz, N_KV, t_len, 1), v.dtype)
    zeros = jnp.zeros((bsz, N_KV, t_len, 128 - HEAD_DIM - 1), v.dtype)
    va = jnp.concatenate([jnp.transpose(v, (0, 2, 1, 3)), ones, zeros], axis=-1)
    return pl.pallas_call(
        functools.partial(_gattn_online_kernel, tk=tk, n_kt=t_len // tk,
                          ctx_qtiles=ctx_len // tq, ctx_kt=ctx_len // tk),
        out_shape=jax.ShapeDtypeStruct((bsz, t_len, _GW), BF16),
        grid=(bsz, N_KV, t_len // tq),
        in_specs=[
            pl.BlockSpec((1, tq, 128), lambda b, gi, qi: (b, qi, gi)),
            pl.BlockSpec((1, 1, HEAD_DIM, t_len), lambda b, gi, qi: (b, gi, 0, 0)),
            pl.BlockSpec((1, 1, t_len, 128), lambda b, gi, qi: (b, gi, 0, 0)),
            pl.BlockSpec((1, tq, 128), lambda b, gi, qi: (b, qi, 4 + gi)),
        ],
        out_specs=pl.BlockSpec((1, tq, 128), lambda b, gi, qi: (b, qi, gi)),
        scratch_shapes=[pltpu.VMEM((2 * tq, 1), F32), pltpu.VMEM((2 * tq, 128), F32)],
        compiler_params=pltpu.CompilerParams(
            dimension_semantics=("arbitrary", "arbitrary", "arbitrary"),
            vmem_limit_bytes=VMEM_LIMIT),
        name="global_attention_online",
    )(g, kt, va, g)


def _gattn_kernel(q_ref, k_ref, vat_ref, gate_ref, o_ref, acc_ref, *, tk, n_kt, ctx_qtiles, ctx_len):
    qi = pl.program_id(1)
    tq = q_ref.shape[1]
    qt = q_ref[0].T
    blk = [qt[hh * HEAD_DIM:(hh + 1) * HEAD_DIM] for hh in range(N_HEADS)]
    zero = jnp.zeros_like(blk[0])
    w = jnp.concatenate([jnp.concatenate([blk[0], blk[1], zero, zero], axis=1),
                         jnp.concatenate([zero, zero, blk[2], blk[3]], axis=1)], axis=0)

    def tile(start, width):
        k = k_ref[0, pl.ds(start, width), :]
        p = jnp.exp(jnp.dot(k, w, preferred_element_type=F32)).astype(BF16)
        for gi in range(N_KV):
            vat = vat_ref[0, gi, :, pl.ds(start, width)]
            acc_ref[gi] += jnp.dot(vat, p[:, gi * 2 * tq:(gi + 1) * 2 * tq],
                                   preferred_element_type=F32)

    acc_ref[...] = jnp.zeros_like(acc_ref)

    is_ctx = qi >= pl.num_programs(1) - ctx_qtiles

    @pl.when(is_ctx)
    def _():
        tile(n_kt * tk - ctx_len, ctx_len)

    @pl.when(jnp.logical_not(is_ctx))
    def _():
        for kt in range(n_kt):
            tile(kt * tk, tk)

    outs = []
    for gi in range(N_KV):
        acc = acc_ref[gi]
        ot = acc[:HEAD_DIM] / acc[HEAD_DIM:HEAD_DIM + 1]
        outs += [ot[:, :tq].T, ot[:, tq:].T]
    out = jnp.concatenate(outs, axis=1)
    o_ref[0] = (out * gate_ref[0].astype(F32)).astype(o_ref.dtype)


def _global_attention_fast(g, gvt, *, ctx_len):
    bsz, t_len, _ = g.shape
    tq = ATTN_TQ_FAST
    assert (t_len - ctx_len) % tq == 0 and ctx_len <= tq
    tk = next(w for w in ATTN_TK_FAST if t_len % w == 0)
    return pl.pallas_call(
        functools.partial(_gattn_kernel, tk=tk, n_kt=t_len // tk, ctx_qtiles=1, ctx_len=ctx_len),
        out_shape=jax.ShapeDtypeStruct((bsz, t_len, _GW), BF16),
        grid=(bsz, pl.cdiv(t_len, tq)),
        in_specs=[
            pl.BlockSpec((1, tq, _GW), lambda b, qi: (b, qi, 0)),
            pl.BlockSpec((1, t_len, KV_WIDTH), lambda b, qi: (b, 0, _GW // KV_WIDTH)),
            pl.BlockSpec((1, N_KV, VAT_ROWS, t_len), lambda b, qi: (b, 0, 0, 0)),
            pl.BlockSpec((1, tq, _GW), lambda b, qi: (b, qi, 2)),
        ],
        out_specs=pl.BlockSpec((1, tq, _GW), lambda b, qi: (b, qi, 0)),
        scratch_shapes=[pltpu.VMEM((N_KV, VAT_ROWS, 2 * tq), F32)],
        compiler_params=pltpu.CompilerParams(
            dimension_semantics=("arbitrary", "arbitrary"), vmem_limit_bytes=VMEM_LIMIT),
        name="global_attention",
    )(g, g, gvt, g)


def _global_attention(g, gvt, score_bound, *, ctx_len):
    return lax.cond(score_bound <= MAX_UNSHIFTED_SCORE,
                    functools.partial(_global_attention_fast, ctx_len=ctx_len),
                    lambda g_, gvt_: _global_attention_online(g_, ctx_len=ctx_len), g, gvt)


def _wattn_kernel(sink_ref, q_ref, k_ref, vat_ref, gate_ref, o_ref, *,
                  chunks_per_step, ctx_len, t_len):
    n_loc = 3 * CHUNK
    lane_head = lax.broadcasted_iota(jnp.int32, (1, N_HEADS * CHUNK), 1) // CHUNK
    sink = jnp.zeros((1, N_HEADS * CHUNK), F32)
    for hh in range(N_HEADS):
        sink = jnp.where(lane_head == hh, sink_ref[hh], sink)
    seq_len = t_len - ctx_len
    k_ctx = k_ref[0, seq_len:t_len, :]
    zero = jnp.zeros((HEAD_DIM, CHUNK), BF16)
    key_row = lax.broadcasted_iota(jnp.int32, (n_loc, CHUNK), 0)
    q_minus_k = lax.broadcasted_iota(jnp.int32, (n_loc, CHUNK), 1) - key_row

    def group(gidx, carry):
        base = gidx * group_size
        rows, starts, st_locs, st_ctxs = [], [], [], []
        for i in range(group_size):
            c = pl.program_id(1) * chunks_per_step + base + i
            rows.append(pl.ds(pl.multiple_of((base + i) * CHUNK, CHUNK), CHUNK))
            qt = q_ref[0, rows[i], :].T
            blk = [qt[hh * HEAD_DIM:(hh + 1) * HEAD_DIM] for hh in range(N_HEADS)]
            w = jnp.concatenate([jnp.concatenate([blk[0], blk[1], zero, zero], axis=1),
                                 jnp.concatenate([zero, zero, blk[2], blk[3]], axis=1)], axis=0)
            start = pl.multiple_of(jnp.clip((c - 1) * CHUNK, 0, t_len - n_loc), CHUNK)
            starts.append(start)
            st_loc = jnp.dot(k_ref[0, pl.ds(start, n_loc), :], w, preferred_element_type=F32)
            st_ctxs.append(jnp.dot(k_ctx, w, preferred_element_type=F32))
            offset = jnp.where(c * CHUNK < seq_len, c * CHUNK - start, 4 * WINDOW + n_loc)
            valid = (jnp.abs(q_minus_k + offset) <= WINDOW) & (key_row < seq_len - start)
            st_locs.append(jnp.concatenate(
                [jnp.where(valid, st_loc[:, hh * CHUNK:(hh + 1) * CHUNK], NEG_INF)
                 for hh in range(N_HEADS)], axis=1))
        p_locs, p_ctxs, e_sinks = [], [], []
        for st_loc, st_ctx in zip(st_locs, st_ctxs):
            m = jnp.maximum(jnp.maximum(jnp.max(st_loc, axis=0, keepdims=True),
                                        jnp.max(st_ctx, axis=0, keepdims=True)), sink)
            p_locs.append(jnp.exp(st_loc - m).astype(BF16))
            p_ctxs.append(jnp.exp(st_ctx - m).astype(BF16))
            e_sinks.append(jnp.exp(sink - m))
        accs = []
        for start, p_loc, p_ctx in zip(starts, p_locs, p_ctxs):
            for gi in range(N_KV):
                cols = slice(gi * 2 * CHUNK, (gi + 1) * 2 * CHUNK)
                accs.append(jnp.dot(vat_ref[0, gi, :, pl.ds(start, n_loc)], p_loc[:, cols],
                                    preferred_element_type=F32)
                            + jnp.dot(vat_ref[0, gi, :, seq_len:t_len], p_ctx[:, cols],
                                      preferred_element_type=F32))
        for i in range(group_size):
            outs = []
            for gi in range(N_KV):
                cols = slice(gi * 2 * CHUNK, (gi + 1) * 2 * CHUNK)
                acc = accs[i * N_KV + gi]
                ot = acc[:HEAD_DIM] / (acc[HEAD_DIM:HEAD_DIM + 1] + e_sinks[i][:, cols])
                outs += [ot[:, :CHUNK].T, ot[:, CHUNK:].T]
            out = jnp.concatenate(outs, axis=1)
            o_ref[0, rows[i], :] = (out * gate_ref[0, rows[i], :].astype(F32)).astype(o_ref.dtype)
        return carry

    group_size = 5 if chunks_per_step % 5 == 0 else 1
    lax.fori_loop(0, chunks_per_step // group_size, group, 0)


def _window_attention(s, svt, sink, *, ctx_len):
    bsz, t_len, _ = s.shape
    nc = t_len // CHUNK
    cps = next(n for n in WINDOW_CHUNKS_PER_STEP if nc % n == 0)
    rows = cps * CHUNK
    return pl.pallas_call(
        functools.partial(_wattn_kernel, chunks_per_step=cps, ctx_len=ctx_len, t_len=t_len),
        out_shape=jax.ShapeDtypeStruct((bsz, t_len, _GW), BF16),
        grid=(bsz, nc // cps),
        in_specs=[
            pl.BlockSpec(memory_space=pltpu.SMEM),
            pl.BlockSpec((1, rows, _GW), lambda b, j: (b, j, 0)),
            pl.BlockSpec((1, t_len, KV_WIDTH), lambda b, j: (b, 0, _GW // KV_WIDTH)),
            pl.BlockSpec((1, N_KV, VAT_ROWS, t_len), lambda b, j: (b, 0, 0, 0)),
            pl.BlockSpec((1, rows, _GW), lambda b, j: (b, j, 2)),
        ],
        out_specs=pl.BlockSpec((1, rows, _GW), lambda b, j: (b, j, 0)),
        compiler_params=pltpu.CompilerParams(
            dimension_semantics=("arbitrary", "arbitrary"), vmem_limit_bytes=VMEM_LIMIT),
        name="window_attention",
    )(sink, s, s, svt, s)


def _outproj_kernel(x_ref, ctx_ref, mod_ref, mlp_ref, yf_ref, yb_ref, rg_ref, ga_ref, sw_ref,
                    rn_ref, bd_ref, w_ref, o_ref, *, seq_len, d_model):
    b = pl.program_id(0)
    t = pl.program_id(1)
    tm = x_ref.shape[1]
    y = yf_ref[0].astype(F32) + yb_ref[0].astype(F32)
    bd = bd_ref[...]
    ms = _group_mean_sq(y, bd)
    ret = (y * lax.rsqrt(ms + RMS_EPS) * rn_ref[...]) * rg_ref[0].astype(F32)
    acc = jnp.dot(mlp_ref[0], w_ref[0:_GW, :], preferred_element_type=F32)
    acc += jnp.dot(ret.astype(BF16), w_ref[_GW:2 * _GW, :], preferred_element_type=F32)
    acc += jnp.dot(ga_ref[0], w_ref[2 * _GW:3 * _GW, :], preferred_element_type=F32)
    acc += jnp.dot(sw_ref[0], w_ref[3 * _GW:4 * _GW, :], preferred_element_type=F32)
    for c in range(tm // CHUNK):
        x, is_ctx = _stream_chunk(x_ref, ctx_ref, t, c, seq_len)
        gate = mod_ref[pl.ds(jnp.where(is_ctx, 2, b), 1), :][:, 2 * d_model:]
        rows = slice(c * CHUNK, (c + 1) * CHUNK)
        o_ref[0, rows, :] = x + gate * acc[rows]


def _outproj(x, ctx, xs, mod, mlp, yf, yb, r, ga, sw, rn, bd, w_out, *, latent_only):
    bsz, seq_len, d = x.shape
    ctx_len = ctx.shape[1]
    t_len = seq_len + ctx_len
    tm = next(n for n in OUTPROJ_TILES if t_len % n == 0)
    out_rows = seq_len if latent_only else t_len
    const2 = lambda b, t: (0, 0)
    tok = lambda b, t: (b, t, 0)
    grp = pl.BlockSpec((1, tm, _GW), tok)
    lat_src, ctx_src = _stream_sources(x, ctx, xs)
    in_place = xs is not None and not latent_only
    return pl.pallas_call(
        functools.partial(_outproj_kernel, seq_len=seq_len, d_model=d),
        out_shape=jax.ShapeDtypeStruct((bsz, out_rows, d), x.dtype),
        grid=(bsz, pl.cdiv(out_rows, tm)),
        in_specs=[
            pl.BlockSpec((1, tm, d), lambda b, t: (b, jnp.minimum(t, (lat_src.shape[1] - 1) // tm), 0)),
            pl.BlockSpec((1, ctx_len, d), lambda b, t: (b, 0, 0)),
            pl.BlockSpec((8, 3 * d), const2),
            grp, grp, grp,
            pl.BlockSpec((1, tm, _GW), lambda b, t: (b, t, 3)),
            grp, grp,
            pl.BlockSpec((1, _GW), const2),
            pl.BlockSpec((_GW, _GW), const2),
            pl.BlockSpec((4 * _GW, d), const2),
        ],
        out_specs=pl.BlockSpec((1, tm, d), tok),
        input_output_aliases={0: 0} if in_place else {},
        compiler_params=pltpu.CompilerParams(
            dimension_semantics=("arbitrary", "arbitrary"), vmem_limit_bytes=VMEM_LIMIT),
        name="outproj",
    )(lat_src, ctx_src, mod, mlp, yf, yb, r, ga, sw, rn, bd, w_out)


def _rope_tables(seq_len, ctx_len):
    rows = seq_len // GRID_W
    half = HEAD_DIM // 2
    inv_freq = 1.0 / (ROPE_BASE ** (jnp.arange(0, half, 2, dtype=F32) / half))
    ang_r = jnp.arange(rows, dtype=F32)[:, None] * inv_freq[None, :]
    ang_c = jnp.arange(GRID_W, dtype=F32)[:, None] * inv_freq[None, :]

    def table(fn):
        by_row = jnp.repeat(fn(ang_r), GRID_W, axis=0)
        by_col = jnp.tile(fn(ang_c), (rows, 1))
        return jnp.concatenate([by_row, by_row, by_col, by_col], axis=-1)

    cos = jnp.concatenate([table(jnp.cos), jnp.ones((ctx_len, HEAD_DIM), F32)], axis=0)
    sin = jnp.concatenate([table(jnp.sin), jnp.zeros((ctx_len, HEAD_DIM), F32)], axis=0)
    first = (jnp.arange(HEAD_DIM) % (half)) < (half // 2)
    sina = jnp.where(first[None, :], -sin, 0.0)
    sinb = jnp.where(first[None, :], 0.0, sin)
    two = lambda a: jnp.concatenate([a, a], axis=-1)
    return two(cos), two(sina), two(sinb)


def kernel(x, c, ctx, c_ctx, norm_gain, w_mod, b_mod, w_in, w_out, mlp_mix, mlp_bias,
           ret_decay_fwd, ret_decay_bwd, ret_norm, attn_q_norm, attn_k_norm,
           swa_q_norm, swa_k_norm, swa_sink):
    bsz, seq_len, d = x.shape
    ctx_len = ctx.shape[1]
    depth = w_in.shape[0]
    assert w_in.shape[2] == IN_WIDTH and w_out.shape[1] == 4 * _GW
    assert ctx_len % ATTN_TQ == 0 and seq_len % ctx_len == 0 and bsz <= 2

    cond = jnp.zeros((8, d), F32).at[:bsz].set(c).at[2].set(c_ctx)
    mod_all = _modulation(cond, w_mod, b_mod)

    cos, sina, sinb = _rope_tables(seq_len, ctx_len)
    lane_group = jnp.arange(_GW) // HEAD_DIM
    bd = ((lane_group[:, None] == lane_group[None, :]).astype(F32) * (1.0 / HEAD_DIM)).astype(BF16)
    w_in_b = w_in.astype(BF16)
    w_out_b = w_out.astype(BF16)
    mix_b = mlp_mix.astype(BF16)
    two = lambda a: jnp.concatenate([a, a], axis=-1)

    xs = None
    for i in range(depth):
        mod = mod_all[i]
        mbias = jnp.repeat(mlp_bias[i].T, HEAD_DIM, axis=1)
        qkg = jnp.stack([two(attn_q_norm[i]) * ATTN_SCALE, two(attn_k_norm[i]),
                         two(swa_q_norm[i]) * ATTN_SCALE, two(swa_k_norm[i])]).astype(F32)
        mlp, r, g, s, gvt, svt = _inproj(x, ctx, xs, mod, norm_gain[i][None, :], w_in_b[i], mix_b[i],
                                         mbias, bd, cos, sina, sinb, qkg)
        lg_f = -jnp.exp(ret_decay_fwd[i].astype(F32))
        lg_b = -jnp.exp(ret_decay_bwd[i].astype(F32))
        decf, decb, vecs = _retention_tables(lg_f, lg_b)
        yf, yb = _retention(r, decf, decb, vecs, ctx_len=ctx_len)
        score_bound = (HEAD_DIM * ATTN_SCALE * jnp.max(jnp.abs(attn_q_norm[i]))
                       * jnp.max(jnp.abs(attn_k_norm[i])))
        ga = _global_attention(g, gvt, score_bound, ctx_len=ctx_len)
        sw = _window_attention(s, svt, swa_sink[i].astype(F32), ctx_len=ctx_len)
        xs = _outproj(x, ctx, xs, mod, mlp, yf, yb, r, ga, sw, ret_norm[i].reshape(1, _GW), bd,
                      w_out_b[i], latent_only=(i == depth - 1))
    return xs
```

```python
import functools

import numpy as np
import jax
import jax.numpy as jnp
from jax import lax
from jax.experimental import pallas as pl
from jax.experimental.pallas import tpu as pltpu

F32 = jnp.float32
BF16 = jnp.bfloat16

GRID_W = 64
CHUNK = 128
WINDOW = 128
HEAD_DIM = 64
N_HEADS = 4
GROUP_WIDTH = N_HEADS * HEAD_DIM
N_KV = 2
KV_WIDTH = N_KV * HEAD_DIM
ROPE_BASE = 10000.0
RMS_EPS = 1e-6
ATTN_SCALE = HEAD_DIM ** -0.5
NEG_INF = -1e30

_GW = GROUP_WIDTH
C_AUV, C_AG = 0, 2 * _GW
C_RQ, C_RK, C_RV, C_RG = 3 * _GW, 4 * _GW, 5 * _GW, 6 * _GW
C_GQ = 7 * _GW
C_GK = C_GQ + _GW
C_GV = C_GK + KV_WIDTH
C_GG = C_GV + KV_WIDTH
C_SQ = C_GG + _GW
C_SK = C_SQ + _GW
C_SV = C_SK + KV_WIDTH
C_SG = C_SV + KV_WIDTH
IN_WIDTH = C_SG + _GW
QKV_WIDTH = 2 * _GW + 2 * KV_WIDTH

INPROJ_TILES = (640, 256)
OUTPROJ_TILES = (1280, 256)
ATTN_TQ = 256
ATTN_TQ_FAST = 512
ATTN_TK_ONLINE = 256
ATTN_TK_FAST = (1664, 256)
MAX_UNSHIFTED_SCORE = 32.0
WINDOW_CHUNKS_PER_STEP = (10, 2, 1)
VAT_ROWS = 80
VMEM_LIMIT = 56 * 1024 * 1024


def _silu(x):
    return x / (1.0 + jnp.exp(-x))


def _head_mask(width, head):
    lane = lax.broadcasted_iota(jnp.int32, (1, width), 1)
    return (lane // HEAD_DIM) == head


def _group_mean_sq(y, bd):
    sq = y * y
    hi = sq.astype(BF16)
    lo = (sq - hi.astype(F32)).astype(BF16)
    return (jnp.dot(hi, bd, preferred_element_type=F32)
            + jnp.dot(lo, bd, preferred_element_type=F32))


def _mod_kernel(cond_ref, w_ref, b_ref, o_ref):
    a = _silu(cond_ref[...])
    o_ref[0] = jnp.dot(a, w_ref[0], preferred_element_type=F32) + b_ref[0]


def _modulation(cond, w_mod, b_mod):
    depth, d, d3 = w_mod.shape
    nblk = d3 // d
    return pl.pallas_call(
        _mod_kernel,
        out_shape=jax.ShapeDtypeStruct((depth, 8, d3), F32),
        grid=(depth, nblk),
        in_specs=[
            pl.BlockSpec((8, d), lambda i, j: (0, 0)),
            pl.BlockSpec((1, d, d), lambda i, j: (i, 0, j)),
            pl.BlockSpec((1, 1, d), lambda i, j: (i, 0, j)),
        ],
        out_specs=pl.BlockSpec((1, 8, d), lambda i, j: (i, 0, j)),
        compiler_params=pltpu.CompilerParams(
            dimension_semantics=("arbitrary", "arbitrary"), vmem_limit_bytes=VMEM_LIMIT),
        name="modulation",
    )(cond, w_mod, b_mod.reshape(depth, 1, d3))


def _stream_chunk(x_ref, ctx_ref, tile, c, seq_len):
    first_row = tile * x_ref.shape[1] + c * CHUNK
    is_ctx = first_row >= seq_len
    ctx_chunk = jnp.clip((first_row - seq_len) // CHUNK, 0, ctx_ref.shape[1] // CHUNK - 1)
    x_ctx = ctx_ref[0, pl.ds(pl.multiple_of(ctx_chunk * CHUNK, CHUNK), CHUNK), :]
    return jnp.where(is_ctx, x_ctx, x_ref[0, c * CHUNK:(c + 1) * CHUNK, :]), is_ctx


def _inproj_kernel(x_ref, ctx_ref, mod_ref, gain_ref, w_ref, mix_ref, mbias_ref, bd_ref,
                   cos_ref, sina_ref, sinb_ref, qkg_ref,
                   mlp_ref, r_ref, g_ref, s_ref, gvt_ref, svt_ref, *, seq_len, d_model):
    b = pl.program_id(0)
    t = pl.program_id(1)
    d = d_model
    tm = x_ref.shape[1]
    gain = gain_ref[...]
    hs = []
    for c in range(tm // CHUNK):
        x, is_ctx = _stream_chunk(x_ref, ctx_ref, t, c, seq_len)
        mod = mod_ref[pl.ds(jnp.where(is_ctx, 2, b), 1), :]
        ms = jnp.mean(x * x, axis=-1, keepdims=True)
        xn = x * lax.rsqrt(ms + RMS_EPS) * gain
        hs.append((xn * (1.0 + mod[:, d:2 * d]) + mod[:, :d]).astype(BF16))
    h = jnp.concatenate(hs, axis=0)

    def proj(lo, width):
        return jnp.dot(h, w_ref[:, lo:lo + width], preferred_element_type=F32)

    bd = bd_ref[...]
    cos = cos_ref[...]
    sina = sina_ref[...]
    sinb = sinb_ref[...]

    def norm_rope_store(out_ref, q, q_ms, k, k_ms, gi):
        qg = qkg_ref[gi:gi + 1, :]
        kg = qkg_ref[gi + 1:gi + 2, :]
        parts = ((q[:, :128], q_ms[:, :128], qg), (q[:, 128:], q_ms[:, 128:], qg), (k, k_ms, kg))
        for j, (y, ms, gain_row) in enumerate(parts):
            yn = y * lax.rsqrt(ms + RMS_EPS) * gain_row
            out = yn * cos + pltpu.roll(yn, 128 - 16, 1) * sina + pltpu.roll(yn, 16, 1) * sinb
            out_ref[0, :, j * 128:(j + 1) * 128] = out.astype(out_ref.dtype)

    pad_row = lax.broadcasted_iota(jnp.int32, (VAT_ROWS - HEAD_DIM, tm), 0)
    ones_then_zeros = jnp.where(pad_row == 0, 1.0, 0.0).astype(BF16)

    def store_v(out_ref, vt_ref, v):
        out_ref[0, :, 384:512] = v.astype(out_ref.dtype)
        vt = v.T.astype(vt_ref.dtype)
        for gi in range(N_KV):
            vt_ref[0, gi, 0:HEAD_DIM, :] = vt[gi * HEAD_DIM:(gi + 1) * HEAD_DIM]
            vt_ref[0, gi, HEAD_DIM:VAT_ROWS, :] = ones_then_zeros

    uv = jax.nn.gelu(proj(C_AUV, 2 * _GW), approximate=True)
    u = uv[:, :_GW]
    v = uv[:, _GW:].astype(BF16)
    ag = _silu(proj(C_AG, _GW))
    gq = proj(C_GQ, _GW)
    g_kv = proj(C_GK, 2 * KV_WIDTH)
    s_kv = proj(C_SK, 2 * KV_WIDTH)
    kk = jnp.concatenate([g_kv[:, :KV_WIDTH], s_kv[:, :KV_WIDTH]], axis=1)

    r_ref[0, :, 0:_GW] = proj(C_RQ, _GW).astype(r_ref.dtype)
    r_ref[0, :, _GW:2 * _GW] = (proj(C_RK, _GW) * ATTN_SCALE).astype(r_ref.dtype)
    r_ref[0, :, 2 * _GW:3 * _GW] = proj(C_RV, _GW).astype(r_ref.dtype)
    r_ref[0, :, 3 * _GW:4 * _GW] = _silu(proj(C_RG, _GW)).astype(r_ref.dtype)

    masks = [_head_mask(_GW, hh) for hh in range(N_HEADS)]
    for c in range(tm // CHUNK):
        rows = slice(c * CHUNK, (c + 1) * CHUNK)
        vc = v[rows]
        sv = mbias_ref[...]
        for hh in range(N_HEADS):
            mixed = jnp.dot(mix_ref[hh], vc, preferred_element_type=F32)
            sv = sv + jnp.where(masks[hh], mixed, 0.0)
        mlp_ref[0, rows, :] = (u[rows] * sv * ag[rows]).astype(mlp_ref.dtype)

    gq_ms = _group_mean_sq(gq, bd)
    kk_ms = _group_mean_sq(kk, bd)
    sq = proj(C_SQ, _GW)
    store_v(g_ref, gvt_ref, g_kv[:, KV_WIDTH:])
    g_ref[0, :, 512:768] = _silu(proj(C_GG, _GW)).astype(g_ref.dtype)
    norm_rope_store(g_ref, gq, gq_ms, kk[:, :128], kk_ms[:, :128], 0)
    sq_ms = _group_mean_sq(sq, bd)
    store_v(s_ref, svt_ref, s_kv[:, KV_WIDTH:])
    s_ref[0, :, 512:768] = _silu(proj(C_SG, _GW)).astype(s_ref.dtype)
    norm_rope_store(s_ref, sq, sq_ms, kk[:, 128:], kk_ms[:, 128:], 2)


def _stream_sources(x, ctx, xs):
    if xs is None:
        return x, ctx
    return xs, xs[:, x.shape[1]:, :]


def _inproj(x, ctx, xs, mod, gain, w_in, mix, mbias, bd, cos, sina, sinb, qkg):
    bsz, seq_len, d = x.shape
    ctx_len = ctx.shape[1]
    t_len = seq_len + ctx_len
    tm = next(n for n in INPROJ_TILES if t_len % n == 0)
    nt = t_len // tm
    const2 = lambda b, t: (0, 0)
    tok = lambda b, t: (b, t, 0)
    tab = lambda b, t: (t, 0)
    lat_src, ctx_src = _stream_sources(x, ctx, xs)
    return pl.pallas_call(
        functools.partial(_inproj_kernel, seq_len=seq_len, d_model=d),
        out_shape=(
            jax.ShapeDtypeStruct((bsz, t_len, _GW), BF16),
            jax.ShapeDtypeStruct((bsz, t_len, 4 * _GW), BF16),
            jax.ShapeDtypeStruct((bsz, t_len, QKV_WIDTH), BF16),
            jax.ShapeDtypeStruct((bsz, t_len, QKV_WIDTH), BF16),
            jax.ShapeDtypeStruct((bsz, N_KV, VAT_ROWS, t_len), BF16),
            jax.ShapeDtypeStruct((bsz, N_KV, VAT_ROWS, t_len), BF16),
        ),
        grid=(bsz, nt),
        in_specs=[
            pl.BlockSpec((1, tm, d), lambda b, t: (b, jnp.minimum(t, (lat_src.shape[1] - 1) // tm), 0)),
            pl.BlockSpec((1, ctx_len, d), lambda b, t: (b, 0, 0)),
            pl.BlockSpec((8, 3 * d), const2),
            pl.BlockSpec((1, d), const2),
            pl.BlockSpec((d, IN_WIDTH), const2),
            pl.BlockSpec((N_HEADS, CHUNK, CHUNK), lambda b, t: (0, 0, 0)),
            pl.BlockSpec((CHUNK, _GW), const2),
            pl.BlockSpec((_GW, _GW), const2),
            pl.BlockSpec((tm, 128), tab),
            pl.BlockSpec((tm, 128), tab),
            pl.BlockSpec((tm, 128), tab),
            pl.BlockSpec((4, 128), const2),
        ],
        out_specs=(
            pl.BlockSpec((1, tm, _GW), tok),
            pl.BlockSpec((1, tm, 4 * _GW), tok),
            pl.BlockSpec((1, tm, QKV_WIDTH), tok),
            pl.BlockSpec((1, tm, QKV_WIDTH), tok),
            pl.BlockSpec((1, N_KV, VAT_ROWS, tm), lambda b, t: (b, 0, 0, t)),
            pl.BlockSpec((1, N_KV, VAT_ROWS, tm), lambda b, t: (b, 0, 0, t)),
        ),
        compiler_params=pltpu.CompilerParams(
            dimension_semantics=("arbitrary", "arbitrary"), vmem_limit_bytes=VMEM_LIMIT),
        name="inproj",
    )(lat_src, ctx_src, mod, gain, w_in, mix, mbias, bd, cos, sina, sinb, qkg)


def _ret_kernel(qf_ref, kf_ref, vf_ref, qb_ref, kb_ref, vb_ref,
                decf_ref, decb_ref, vec_ref, yf_ref, yb_ref, sf_ref, sb_ref):
    @pl.when(pl.program_id(1) == 0)
    def _():
        sf_ref[...] = jnp.zeros_like(sf_ref)
        sb_ref[...] = jnp.zeros_like(sb_ref)

    n = qf_ref.shape[1] // CHUNK
    masks = [_head_mask(_GW, hh) for hh in range(N_HEADS)]
    nt = (((1,), (1,)), ((), ()))
    tn = (((0,), (0,)), ((), ()))
    fwd = (qf_ref, kf_ref, vf_ref, decf_ref, 0, yf_ref, sf_ref)
    bwd = (qb_ref, kb_ref, vb_ref, decb_ref, 3, yb_ref, sb_ref)
    work = []
    for i in range(n):
        work.append((fwd, slice(i * CHUNK, (i + 1) * CHUNK)))
        work.append((bwd, slice((n - 1 - i) * CHUNK, (n - i) * CHUNK)))

    scores, kvs = [], []
    for (q_ref, k_ref, v_ref, _, vi, _, _), rows in work:
        q, k, v = q_ref[0, rows, :], k_ref[0, rows, :], v_ref[0, rows, :]
        scores.append([lax.dot_general(jnp.where(m, q, jnp.zeros_like(q)), k, nt,
                                       preferred_element_type=F32) for m in masks])
        kd = (k.astype(F32) * vec_ref[vi + 1]).astype(BF16)
        kvs.append(lax.dot_general(kd, v, tn, preferred_element_type=F32))
    intras = []
    for ((_, _, v_ref, dec_ref, _, _, _), rows), sc in zip(work, scores):
        v = v_ref[0, rows, :]
        y = jnp.zeros((CHUNK, _GW), F32)
        for hh in range(N_HEADS):
            p = (sc[hh] * dec_ref[hh]).astype(BF16)
            y = y + jnp.where(masks[hh], jnp.dot(p, v, preferred_element_type=F32), 0.0)
        intras.append(y)
    row_head = lax.broadcasted_iota(jnp.int32, (_GW, _GW), 0) // HEAD_DIM
    col_head = lax.broadcasted_iota(jnp.int32, (_GW, _GW), 1) // HEAD_DIM
    for ((q_ref, _, _, _, vi, y_ref, state_ref), rows), intra, kv in zip(work, intras, kvs):
        state = state_ref[...]
        inter = jnp.dot(q_ref[0, rows, :], state.astype(BF16), preferred_element_type=F32)
        y_ref[0, rows, :] = (intra + inter * vec_ref[vi]).astype(y_ref.dtype)
        state_ref[...] = state * vec_ref[vi + 2, 0:1, :] + jnp.where(row_head == col_head, kv, 0.0)


def _retention(r, decf, decb, vecs, *, ctx_len):
    bsz, t_len, _ = r.shape
    rows = ctx_len
    nb = t_len // rows

    def fwd_block(t):
        return jnp.where(t < 1, nb - 1, t - 1)

    def bwd_block(t):
        return jnp.where(t < 1, nb - 1, nb - 1 - t)

    def fwd(col):
        return pl.BlockSpec((1, rows, _GW), lambda b, t: (b, fwd_block(t), col))

    def bwd(col):
        return pl.BlockSpec((1, rows, _GW), lambda b, t: (b, bwd_block(t), col))

    return pl.pallas_call(
        _ret_kernel,
        out_shape=(jax.ShapeDtypeStruct((bsz, t_len, _GW), BF16),
                   jax.ShapeDtypeStruct((bsz, t_len, _GW), BF16)),
        grid=(bsz, nb),
        in_specs=[fwd(0), fwd(1), fwd(2), bwd(0), bwd(1), bwd(2),
                  pl.BlockSpec((N_HEADS, CHUNK, CHUNK), lambda b, t: (0, 0, 0)),
                  pl.BlockSpec((N_HEADS, CHUNK, CHUNK), lambda b, t: (0, 0, 0)),
                  pl.BlockSpec((6, CHUNK, _GW), lambda b, t: (0, 0, 0))],
        out_specs=(pl.BlockSpec((1, rows, _GW), lambda b, t: (b, fwd_block(t), 0)),
                   pl.BlockSpec((1, rows, _GW), lambda b, t: (b, bwd_block(t), 0))),
        scratch_shapes=[pltpu.VMEM((_GW, _GW), F32), pltpu.VMEM((_GW, _GW), F32)],
        compiler_params=pltpu.CompilerParams(
            dimension_semantics=("arbitrary", "arbitrary"), vmem_limit_bytes=VMEM_LIMIT),
        name="retention",
    )(r, r, r, r, r, r, decf, decb, vecs)


def _retention_tables(lg_f, lg_b):
    pos = jnp.arange(CHUNK, dtype=F32)
    diff = pos[:, None] - pos[None, :]
    keep_f = diff >= 0
    dec_f = jnp.where(keep_f, jnp.exp(lg_f[:, None, None] * jnp.where(keep_f, diff, 0.0)), 0.0)
    keep_b = diff < 0
    dec_b = jnp.where(keep_b, jnp.exp(lg_b[:, None, None] * jnp.where(keep_b, -diff, 0.0)), 0.0)

    def lanes(tab):
        return jnp.repeat(tab.T, HEAD_DIM, axis=1)

    qdec_f = lanes(jnp.exp(lg_f[:, None] * (pos + 1.0)))
    kdec_f = lanes(jnp.exp(lg_f[:, None] * (CHUNK - 1.0 - pos)))
    qdec_b = lanes(jnp.exp(lg_b[:, None] * (CHUNK - pos)))
    kdec_b = lanes(jnp.exp(lg_b[:, None] * pos))
    cdec_f = jnp.broadcast_to(jnp.repeat(jnp.exp(lg_f * CHUNK), HEAD_DIM)[None, :], (CHUNK, _GW))
    cdec_b = jnp.broadcast_to(jnp.repeat(jnp.exp(lg_b * CHUNK), HEAD_DIM)[None, :], (CHUNK, _GW))
    vecs = jnp.stack([qdec_f, kdec_f, cdec_f, qdec_b, kdec_b, cdec_b]).astype(F32)
    return dec_f.astype(F32), dec_b.astype(F32), vecs


def _gattn_online_kernel(q_ref, kt_ref, va_ref, gate_ref, o_ref, m_ref, acc_ref, *,
                         tk, n_kt, ctx_qtiles, ctx_kt):
    qi = pl.program_id(2)
    tq = q_ref.shape[1]
    q = q_ref[0]
    q2 = jnp.concatenate([q[:, :HEAD_DIM], q[:, HEAD_DIM:]], axis=0)
    m_ref[...] = jnp.full(m_ref.shape, NEG_INF, F32)
    acc_ref[...] = jnp.zeros_like(acc_ref)

    def body(kt, carry):
        start = pl.multiple_of(kt * tk, tk)
        k = kt_ref[0, 0, :, pl.ds(start, tk)]
        v = va_ref[0, 0, pl.ds(start, tk), :]
        s = jnp.dot(q2, k, preferred_element_type=F32)
        m_prev = m_ref[...]
        m_new = jnp.maximum(m_prev, jnp.max(s, axis=-1, keepdims=True))
        p = jnp.exp(s - m_new).astype(BF16)
        acc_ref[...] = (jnp.exp(m_prev - m_new) * acc_ref[...]
                        + jnp.dot(p, v, preferred_element_type=F32))
        m_ref[...] = m_new
        return carry

    is_ctx = qi >= pl.num_programs(2) - ctx_qtiles
    lax.fori_loop(jnp.where(is_ctx, n_kt - ctx_kt, 0), n_kt, body, 0)
    acc = acc_ref[...]
    o = acc[:, :HEAD_DIM] / acc[:, HEAD_DIM:HEAD_DIM + 1]
    out = jnp.concatenate([o[:tq], o[tq:]], axis=1)
    o_ref[0] = (out * gate_ref[0].astype(F32)).astype(o_ref.dtype)


def _global_attention_online(g, *, ctx_len):
    bsz, t_len, _ = g.shape
    tq, tk = ATTN_TQ, ATTN_TK_ONLINE
    k = g[:, :, _GW:_GW + KV_WIDTH].reshape(bsz, t_len, N_KV, HEAD_DIM)
    v = g[:, :, _GW + KV_WIDTH:_GW + 2 * KV_WIDTH].reshape(bsz, t_len, N_KV, HEAD_DIM)
    kt = jnp.transpose(k, (0, 2, 3, 1))
    ones = jnp.ones((bsz, N_KV, t_len, 1), v.dtype)
    zeros = jnp.zeros((bsz, N_KV, t_len, 128 - HEAD_DIM - 1), v.dtype)
    va = jnp.concatenate([jnp.transpose(v, (0, 2, 1, 3)), ones, zeros], axis=-1)
    return pl.pallas_call(
        functools.partial(_gattn_online_kernel, tk=tk, n_kt=t_len // tk,
                          ctx_qtiles=ctx_len // tq, ctx_kt=ctx_len // tk),
        out_shape=jax.ShapeDtypeStruct((bsz, t_len, _GW), BF16),
        grid=(bsz, N_KV, t_len // tq),
        in_specs=[
            pl.BlockSpec((1, tq, 128), lambda b, gi, qi: (b, qi, gi)),
            pl.BlockSpec((1, 1, HEAD_DIM, t_len), lambda b, gi, qi: (b, gi, 0, 0)),
            pl.BlockSpec((1, 1, t_len, 128), lambda b, gi, qi: (b, gi, 0, 0)),
            pl.BlockSpec((1, tq, 128), lambda b, gi, qi: (b, qi, 4 + gi)),
        ],
        out_specs=pl.BlockSpec((1, tq, 128), lambda b, gi, qi: (b, qi, gi)),
        scratch_shapes=[pltpu.VMEM((2 * tq, 1), F32), pltpu.VMEM((2 * tq, 128), F32)],
        compiler_params=pltpu.CompilerParams(
            dimension_semantics=("arbitrary", "arbitrary", "arbitrary"),
            vmem_limit_bytes=VMEM_LIMIT),
        name="global_attention_online",
    )(g, kt, va, g)


def _gattn_kernel(q_ref, k_ref, vat_ref, gate_ref, o_ref, acc_ref, *, tk, n_kt, ctx_qtiles, ctx_len):
    qi = pl.program_id(1)
    tq = q_ref.shape[1]
    qt = q_ref[0].T
    blk = [qt[hh * HEAD_DIM:(hh + 1) * HEAD_DIM] for hh in range(N_HEADS)]
    zero = jnp.zeros_like(blk[0])
    w = jnp.concatenate([jnp.concatenate([blk[0], blk[1], zero, zero], axis=1),
                         jnp.concatenate([zero, zero, blk[2], blk[3]], axis=1)], axis=0)

    def tile(start, width):
        k = k_ref[0, pl.ds(start, width), :]
        p = jnp.exp(jnp.dot(k, w, preferred_element_type=F32)).astype(BF16)
        for gi in range(N_KV):
            vat = vat_ref[0, gi, :, pl.ds(start, width)]
            acc_ref[gi] += jnp.dot(vat, p[:, gi * 2 * tq:(gi + 1) * 2 * tq],
                                   preferred_element_type=F32)

    acc_ref[...] = jnp.zeros_like(acc_ref)

    is_ctx = qi >= pl.num_programs(1) - ctx_qtiles

    @pl.when(is_ctx)
    def _():
        tile(n_kt * tk - ctx_len, ctx_len)

    @pl.when(jnp.logical_not(is_ctx))
    def _():
        for kt in range(n_kt):
            tile(kt * tk, tk)

    outs = []
    for gi in range(N_KV):
        acc = acc_ref[gi]
        ot = acc[:HEAD_DIM] / acc[HEAD_DIM:HEAD_DIM + 1]
        outs += [ot[:, :tq].T, ot[:, tq:].T]
    out = jnp.concatenate(outs, axis=1)
    o_ref[0] = (out * gate_ref[0].astype(F32)).astype(o_ref.dtype)


def _global_attention_fast(g, gvt, *, ctx_len):
    bsz, t_len, _ = g.shape
    tq = ATTN_TQ_FAST
    assert (t_len - ctx_len) % tq == 0 and ctx_len <= tq
    tk = next(w for w in ATTN_TK_FAST if t_len % w == 0)
    return pl.pallas_call(
        functools.partial(_gattn_kernel, tk=tk, n_kt=t_len // tk, ctx_qtiles=1, ctx_len=ctx_len),
        out_shape=jax.ShapeDtypeStruct((bsz, t_len, _GW), BF16),
        grid=(bsz, pl.cdiv(t_len, tq)),
        in_specs=[
            pl.BlockSpec((1, tq, _GW), lambda b, qi: (b, qi, 0)),
            pl.BlockSpec((1, t_len, KV_WIDTH), lambda b, qi: (b, 0, _GW // KV_WIDTH)),
            pl.BlockSpec((1, N_KV, VAT_ROWS, t_len), lambda b, qi: (b, 0, 0, 0)),
            pl.BlockSpec((1, tq, _GW), lambda b, qi: (b, qi, 2)),
        ],
        out_specs=pl.BlockSpec((1, tq, _GW), lambda b, qi: (b, qi, 0)),
        scratch_shapes=[pltpu.VMEM((N_KV, VAT_ROWS, 2 * tq), F32)],
        compiler_params=pltpu.CompilerParams(
            dimension_semantics=("arbitrary", "arbitrary"), vmem_limit_bytes=VMEM_LIMIT),
        name="global_attention",
    )(g, g, gvt, g)


def _global_attention(g, gvt, score_bound, *, ctx_len):
    return lax.cond(score_bound <= MAX_UNSHIFTED_SCORE,
                    functools.partial(_global_attention_fast, ctx_len=ctx_len),
                    lambda g_, gvt_: _global_attention_online(g_, ctx_len=ctx_len), g, gvt)


def _wattn_kernel(sink_ref, q_ref, k_ref, vat_ref, gate_ref, o_ref, *,
                  chunks_per_step, ctx_len, t_len):
    n_loc = 3 * CHUNK
    lane_head = lax.broadcasted_iota(jnp.int32, (1, N_HEADS * CHUNK), 1) // CHUNK
    sink = jnp.zeros((1, N_HEADS * CHUNK), F32)
    for hh in range(N_HEADS):
        sink = jnp.where(lane_head == hh, sink_ref[hh], sink)
    seq_len = t_len - ctx_len
    k_ctx = k_ref[0, seq_len:t_len, :]
    zero = jnp.zeros((HEAD_DIM, CHUNK), BF16)
    key_row = lax.broadcasted_iota(jnp.int32, (n_loc, CHUNK), 0)
    q_minus_k = lax.broadcasted_iota(jnp.int32, (n_loc, CHUNK), 1) - key_row

    def group(gidx, carry):
        base = gidx * group_size
        rows, starts, st_locs, st_ctxs = [], [], [], []
        for i in range(group_size):
            c = pl.program_id(1) * chunks_per_step + base + i
            rows.append(pl.ds(pl.multiple_of((base + i) * CHUNK, CHUNK), CHUNK))
            qt = q_ref[0, rows[i], :].T
            blk = [qt[hh * HEAD_DIM:(hh + 1) * HEAD_DIM] for hh in range(N_HEADS)]
            w = jnp.concatenate([jnp.concatenate([blk[0], blk[1], zero, zero], axis=1),
                                 jnp.concatenate([zero, zero, blk[2], blk[3]], axis=1)], axis=0)
            start = pl.multiple_of(jnp.clip((c - 1) * CHUNK, 0, t_len - n_loc), CHUNK)
            starts.append(start)
            st_loc = jnp.dot(k_ref[0, pl.ds(start, n_loc), :], w, preferred_element_type=F32)
            st_ctxs.append(jnp.dot(k_ctx, w, preferred_element_type=F32))
            offset = jnp.where(c * CHUNK < seq_len, c * CHUNK - start, 4 * WINDOW + n_loc)
            valid = (jnp.abs(q_minus_k + offset) <= WINDOW) & (key_row < seq_len - start)
            st_locs.append(jnp.concatenate(
                [jnp.where(valid, st_loc[:, hh * CHUNK:(hh + 1) * CHUNK], NEG_INF)
                 for hh in range(N_HEADS)], axis=1))
        p_locs, p_ctxs, e_sinks = [], [], []
        for st_loc, st_ctx in zip(st_locs, st_ctxs):
            m = jnp.maximum(jnp.maximum(jnp.max(st_loc, axis=0, keepdims=True),
                                        jnp.max(st_ctx, axis=0, keepdims=True)), sink)
            p_locs.append(jnp.exp(st_loc - m).astype(BF16))
            p_ctxs.append(jnp.exp(st_ctx - m).astype(BF16))
            e_sinks.append(jnp.exp(sink - m))
        accs = []
        for start, p_loc, p_ctx in zip(starts, p_locs, p_ctxs):
            for gi in range(N_KV):
                cols = slice(gi * 2 * CHUNK, (gi + 1) * 2 * CHUNK)
                accs.append(jnp.dot(vat_ref[0, gi, :, pl.ds(start, n_loc)], p_loc[:, cols],
                                    preferred_element_type=F32)
                            + jnp.dot(vat_ref[0, gi, :, seq_len:t_len], p_ctx[:, cols],
                                      preferred_element_type=F32))
        for i in range(group_size):
            outs = []
            for gi in range(N_KV):
                cols = slice(gi * 2 * CHUNK, (gi + 1) * 2 * CHUNK)
                acc = accs[i * N_KV + gi]
                ot = acc[:HEAD_DIM] / (acc[HEAD_DIM:HEAD_DIM + 1] + e_sinks[i][:, cols])
                outs += [ot[:, :CHUNK].T, ot[:, CHUNK:].T]
            out = jnp.concatenate(outs, axis=1)
            o_ref[0, rows[i], :] = (out * gate_ref[0, rows[i], :].astype(F32)).astype(o_ref.dtype)
        return carry

    group_size = 5 if chunks_per_step % 5 == 0 else 1
    lax.fori_loop(0, chunks_per_step // group_size, group, 0)


def _window_attention(s, svt, sink, *, ctx_len):
    bsz, t_len, _ = s.shape
    nc = t_len // CHUNK
    cps = next(n for n in WINDOW_CHUNKS_PER_STEP if nc % n == 0)
    rows = cps * CHUNK
    return pl.pallas_call(
        functools.partial(_wattn_kernel, chunks_per_step=cps, ctx_len=ctx_len, t_len=t_len),
        out_shape=jax.ShapeDtypeStruct((bsz, t_len, _GW), BF16),
        grid=(bsz, nc // cps),
        in_specs=[
            pl.BlockSpec(memory_space=pltpu.SMEM),
            pl.BlockSpec((1, rows, _GW), lambda b, j: (b, j, 0)),
            pl.BlockSpec((1, t_len, KV_WIDTH), lambda b, j: (b, 0, _GW // KV_WIDTH)),
            pl.BlockSpec((1, N_KV, VAT_ROWS, t_len), lambda b, j: (b, 0, 0, 0)),
            pl.BlockSpec((1, rows, _GW), lambda b, j: (b, j, 2)),
        ],
        out_specs=pl.BlockSpec((1, rows, _GW), lambda b, j: (b, j, 0)),
        compiler_params=pltpu.CompilerParams(
            dimension_semantics=("arbitrary", "arbitrary"), vmem_limit_bytes=VMEM_LIMIT),
        name="window_attention",
    )(sink, s, s, svt, s)


def _outproj_kernel(x_ref, ctx_ref, mod_ref, mlp_ref, yf_ref, yb_ref, rg_ref, ga_ref, sw_ref,
                    rn_ref, bd_ref, w_ref, o_ref, *, seq_len, d_model):
    b = pl.program_id(0)
    t = pl.program_id(1)
    tm = x_ref.shape[1]
    y = yf_ref[0].astype(F32) + yb_ref[0].astype(F32)
    bd = bd_ref[...]
    ms = _group_mean_sq(y, bd)
    ret = (y * lax.rsqrt(ms + RMS_EPS) * rn_ref[...]) * rg_ref[0].astype(F32)
    acc = jnp.dot(mlp_ref[0], w_ref[0:_GW, :], preferred_element_type=F32)
    acc += jnp.dot(ret.astype(BF16), w_ref[_GW:2 * _GW, :], preferred_element_type=F32)
    acc += jnp.dot(ga_ref[0], w_ref[2 * _GW:3 * _GW, :], preferred_element_type=F32)
    acc += jnp.dot(sw_ref[0], w_ref[3 * _GW:4 * _GW, :], preferred_element_type=F32)
    for c in range(tm // CHUNK):
        x, is_ctx = _stream_chunk(x_ref, ctx_ref, t, c, seq_len)
        gate = mod_ref[pl.ds(jnp.where(is_ctx, 2, b), 1), :][:, 2 * d_model:]
        rows = slice(c * CHUNK, (c + 1) * CHUNK)
        o_ref[0, rows, :] = x + gate * acc[rows]


def _outproj(x, ctx, xs, mod, mlp, yf, yb, r, ga, sw, rn, bd, w_out, *, latent_only):
    bsz, seq_len, d = x.shape
    ctx_len = ctx.shape[1]
    t_len = seq_len + ctx_len
    tm = next(n for n in OUTPROJ_TILES if t_len % n == 0)
    out_rows = seq_len if latent_only else t_len
    const2 = lambda b, t: (0, 0)
    tok = lambda b, t: (b, t, 0)
    grp = pl.BlockSpec((1, tm, _GW), tok)
    lat_src, ctx_src = _stream_sources(x, ctx, xs)
    in_place = xs is not None and not latent_only
    return pl.pallas_call(
        functools.partial(_outproj_kernel, seq_len=seq_len, d_model=d),
        out_shape=jax.ShapeDtypeStruct((bsz, out_rows, d), x.dtype),
        grid=(bsz, pl.cdiv(out_rows, tm)),
        in_specs=[
            pl.BlockSpec((1, tm, d), lambda b, t: (b, jnp.minimum(t, (lat_src.shape[1] - 1) // tm), 0)),
            pl.BlockSpec((1, ctx_len, d), lambda b, t: (b, 0, 0)),
            pl.BlockSpec((8, 3 * d), const2),
            grp, grp, grp,
            pl.BlockSpec((1, tm, _GW), lambda b, t: (b, t, 3)),
            grp, grp,
            pl.BlockSpec((1, _GW), const2),
            pl.BlockSpec((_GW, _GW), const2),
            pl.BlockSpec((4 * _GW, d), const2),
        ],
        out_specs=pl.BlockSpec((1, tm, d), tok),
        input_output_aliases={0: 0} if in_place else {},
        compiler_params=pltpu.CompilerParams(
            dimension_semantics=("arbitrary", "arbitrary"), vmem_limit_bytes=VMEM_LIMIT),
        name="outproj",
    )(lat_src, ctx_src, mod, mlp, yf, yb, r, ga, sw, rn, bd, w_out)


def _rope_tables(seq_len, ctx_len):
    rows = seq_len // GRID_W
    half = HEAD_DIM // 2
    inv_freq = 1.0 / (ROPE_BASE ** (jnp.arange(0, half, 2, dtype=F32) / half))
    ang_r = jnp.arange(rows, dtype=F32)[:, None] * inv_freq[None, :]
    ang_c = jnp.arange(GRID_W, dtype=F32)[:, None] * inv_freq[None, :]

    def table(fn):
        by_row = jnp.repeat(fn(ang_r), GRID_W, axis=0)
        by_col = jnp.tile(fn(ang_c), (rows, 1))
        return jnp.concatenate([by_row, by_row, by_col, by_col], axis=-1)

    cos = jnp.concatenate([table(jnp.cos), jnp.ones((ctx_len, HEAD_DIM), F32)], axis=0)
    sin = jnp.concatenate([table(jnp.sin), jnp.zeros((ctx_len, HEAD_DIM), F32)], axis=0)
    first = (jnp.arange(HEAD_DIM) % (half)) < (half // 2)
    sina = jnp.where(first[None, :], -sin, 0.0)
    sinb = jnp.where(first[None, :], 0.0, sin)
    two = lambda a: jnp.concatenate([a, a], axis=-1)
    return two(cos), two(sina), two(sinb)


def kernel(x, c, ctx, c_ctx, norm_gain, w_mod, b_mod, w_in, w_out, mlp_mix, mlp_bias,
           ret_decay_fwd, ret_decay_bwd, ret_norm, attn_q_norm, attn_k_norm,
           swa_q_norm, swa_k_norm, swa_sink):
    bsz, seq_len, d = x.shape
    ctx_len = ctx.shape[1]
    depth = w_in.shape[0]
    assert w_in.shape[2] == IN_WIDTH and w_out.shape[1] == 4 * _GW
    assert ctx_len % ATTN_TQ == 0 and seq_len % ctx_len == 0 and bsz <= 2

    cond = jnp.zeros((8, d), F32).at[:bsz].set(c).at[2].set(c_ctx)
    mod_all = _modulation(cond, w_mod, b_mod)

    cos, sina, sinb = _rope_tables(seq_len, ctx_len)
    lane_group = jnp.arange(_GW) // HEAD_DIM
    bd = ((lane_group[:, None] == lane_group[None, :]).astype(F32) * (1.0 / HEAD_DIM)).astype(BF16)
    w_in_b = w_in.astype(BF16)
    w_out_b = w_out.astype(BF16)
    mix_b = mlp_mix.astype(BF16)
    two = lambda a: jnp.concatenate([a, a], axis=-1)

    xs = None
    for i in range(depth):
        mod = mod_all[i]
        mbias = jnp.repeat(mlp_bias[i].T, HEAD_DIM, axis=1)
        qkg = jnp.stack([two(attn_q_norm[i]) * ATTN_SCALE, two(attn_k_norm[i]),
                         two(swa_q_norm[i]) * ATTN_SCALE, two(swa_k_norm[i])]).astype(F32)
        mlp, r, g, s, gvt, svt = _inproj(x, ctx, xs, mod, norm_gain[i][None, :], w_in_b[i], mix_b[i],
                                         mbias, bd, cos, sina, sinb, qkg)
        lg_f = -jnp.exp(ret_decay_fwd[i].astype(F32))
        lg_b = -jnp.exp(ret_decay_bwd[i].astype(F32))
        decf, decb, vecs = _retention_tables(lg_f, lg_b)
        yf, yb = _retention(r, decf, decb, vecs, ctx_len=ctx_len)
        score_bound = (HEAD_DIM * ATTN_SCALE * jnp.max(jnp.abs(attn_q_norm[i]))
                       * jnp.max(jnp.abs(attn_k_norm[i])))
        ga = _global_attention(g, gvt, score_bound, ctx_len=ctx_len)
        sw = _window_attention(s, svt, swa_sink[i].astype(F32), ctx_len=ctx_len)
        xs = _outproj(x, ctx, xs, mod, mlp, yf, yb, r, ga, sw, ret_norm[i].reshape(1, _GW), bd,
                      w_out_b[i], latent_only=(i == depth - 1))
    return xs
```
